```python
import math
import jax, jax.numpy as jnp
from jax import lax
import numpy as np

D_MODEL = 1024
BATCH = 4
SEQ = 8192
DEPTH = 2

N_MIXERS = 2
N_META = 16
RMS_EPS = 1e-6
SB_HEADS = 16
SB_HEAD_DIM = D_MODEL // SB_HEADS
SB_BLOCK = 128
POOL_WINDOWS = (2, 4, 8, 16)
POOL_GROUP = D_MODEL // len(POOL_WINDOWS)
PEER_HEADS = 8
PEER_N_KEYS = 128
PEER_N_EXPERTS = PEER_N_KEYS * PEER_N_KEYS
PEER_TOPK = 16
PEER_QUERY_DIM = 256
PEER_HALF = PEER_QUERY_DIM // 2
PEER_CHUNK = 256

N_A_LAYERS = (DEPTH + N_MIXERS - 1) // N_MIXERS
N_B_LAYERS = DEPTH // N_MIXERS

kernel_name = "sb_pool_peer_hybrid"


def rms_norm(x, gain):
    x32 = x.astype(jnp.float32)
    y = x32 * lax.rsqrt(jnp.mean(x32 * x32, axis=-1, keepdims=True) + RMS_EPS)
    return (y * gain.astype(jnp.float32)).astype(x.dtype)


def stick_breaking_attention(xn, w_qkv, q_gain, k_gain, w_o):
    B, L, D = xn.shape
    qkv = xn @ w_qkv
    q, k, v = jnp.split(qkv, 3, axis=-1)
    q = rms_norm(q.reshape(B, L, SB_HEADS, SB_HEAD_DIM), q_gain)
    k = rms_norm(k.reshape(B, L, SB_HEADS, SB_HEAD_DIM), k_gain)
    v = v.reshape(B, L, SB_HEADS, SB_HEAD_DIM)
    pad = SB_BLOCK - N_META
    padf = lambda a: jnp.pad(a, ((0, 0), (pad, 0), (0, 0), (0, 0)))
    q, k, v = padf(q), padf(k), padf(v)
    Lp = L + pad
    nb = Lp // SB_BLOCK
    kh = k.transpose(0, 2, 1, 3)
    vh = v.transpose(0, 2, 1, 3)
    qb = q.reshape(B, nb, SB_BLOCK, SB_HEADS, SB_HEAD_DIM).transpose(1, 0, 3, 2, 4)
    key_pos = jnp.arange(Lp)
    scale = 1.0 / math.sqrt(SB_HEAD_DIM)

    def block(args):
        bi, qblk = args
        q_pos = bi * SB_BLOCK + jnp.arange(SB_BLOCK)
        z = jnp.einsum('bhqd,bhkd->bhqk', qblk, kh).astype(jnp.float32) * scale
        mask = (key_pos[None, :] < q_pos[:, None]) & (key_pos[None, :] >= pad)
        log_keep = jnp.where(mask, jax.nn.log_sigmoid(-z), 0.0)
        log_after = lax.cumsum(log_keep, axis=3, reverse=True) - log_keep
        a = jnp.where(mask, jnp.exp(jax.nn.log_sigmoid(z) + log_after), 0.0)
        return jnp.einsum('bhqk,bhkd->bhqd', a.astype(vh.dtype), vh)

    o = lax.map(block, (jnp.arange(nb), qb))
    o = o.transpose(1, 0, 3, 2, 4).reshape(B, Lp, D)[:, pad:]
    return o @ w_o


def pool_mixer(xn, w_grp, scale):
    B, L, D = xn.shape
    x32 = xn.astype(jnp.float32)
    c = jnp.concatenate([jnp.zeros((B, 1, D), jnp.float32), jnp.cumsum(x32, axis=1)], axis=1)
    t = jnp.arange(L)
    outs = []
    for g, w in enumerate(POOL_WINDOWS):
        lo, hi = g * POOL_GROUP, (g + 1) * POOL_GROUP
        cg = c[:, :, lo:hi]
        shifted = jnp.concatenate([jnp.zeros((B, w - 1, POOL_GROUP), jnp.float32), cg[:, :L - w + 1]], axis=1)
        cnt = jnp.minimum(t + 1, w).astype(jnp.float32)
        mean = (cg[:, 1:] - shifted) / cnt[None, :, None]
        outs.append(mean - x32[:, :, lo:hi])
    pooled = jnp.stack(outs, axis=2).astype(xn.dtype)
    y = jnp.einsum('blgc,gcd->blgd', pooled, w_grp).reshape(B, L, D)
    return y * scale


def peer_ffn(xn, w_q, subkeys, u_tab, v_tab):
    B, L, D = xn.shape
    T = B * L
    nc = -(-T // PEER_CHUNK)
    xt = jnp.pad(xn.reshape(T, D), ((0, nc * PEER_CHUNK - T), (0, 0))).reshape(nc, PEER_CHUNK, D)

    def chunk(xc):
        C = xc.shape[0]
        q = (xc @ w_q).reshape(C, PEER_HEADS, 2, PEER_HALF)
        s = jnp.einsum('chpd,pnd->chpn', q, subkeys).astype(jnp.float32)
        s_top, i_top = lax.top_k(s, PEER_TOPK)
        cand = (s_top[:, :, 0, :, None] + s_top[:, :, 1, None, :]).reshape(C, PEER_HEADS, PEER_TOPK * PEER_TOPK)
        cidx = (i_top[:, :, 0, :, None] * PEER_N_KEYS + i_top[:, :, 1, None, :]).reshape(C, PEER_HEADS, PEER_TOPK * PEER_TOPK)
        sc, pos = lax.top_k(cand, PEER_TOPK)
        idx = jnp.take_along_axis(cidx, pos, axis=-1)
        g = jax.nn.softmax(sc, axis=-1)
        u = jnp.take(u_tab, idx, axis=0)
        act = jax.nn.gelu(jnp.einsum('chkd,cd->chk', u, xc).astype(jnp.float32), approximate=False)
        v = jnp.take(v_tab, idx, axis=0)
        return jnp.einsum('chk,chkd->cd', (g * act).astype(xc.dtype), v)

    out = lax.map(chunk, xt).reshape(nc * PEER_CHUNK, D)[:T]
    return out.reshape(B, L, D)


def setup_inputs(seed: int = 0) -> dict:
    key = jax.random.key(seed)
    ks = jax.random.split(key, 16)
    D = D_MODEL
    nrm = lambda k, shape, s: jax.random.normal(k, shape, jnp.float32) * s
    return {
        "x": nrm(ks[0], (BATCH, SEQ, D), 1.0),
        "meta": nrm(ks[1], (N_META, D), 1.0),
        "norm_mix": 1.0 + nrm(ks[2], (DEPTH, D), 0.05),
        "norm_ffn": 1.0 + nrm(ks[3], (DEPTH, D), 0.05),
        "sb_w_qkv": nrm(ks[4], (N_A_LAYERS, D, 3 * D), D ** -0.5),
        "sb_q_gain": 1.0 + nrm(ks[5], (N_A_LAYERS, SB_HEAD_DIM), 0.05),
        "sb_k_gain": 1.0 + nrm(ks[6], (N_A_LAYERS, SB_HEAD_DIM), 0.05),
        "sb_w_o": nrm(ks[7], (N_A_LAYERS, D, D), D ** -0.5),
        "pool_w": nrm(ks[8], (N_B_LAYERS, len(POOL_WINDOWS), POOL_GROUP, POOL_GROUP), POOL_GROUP ** -0.5),
        "pool_scale": 1.0 + nrm(ks[9], (N_B_LAYERS, D), 0.05),
        "peer_w_q": nrm(ks[10], (DEPTH, D, PEER_HEADS * PEER_QUERY_DIM), D ** -0.5),
        "peer_subkeys": nrm(ks[11], (DEPTH, 2, PEER_N_KEYS, PEER_HALF), PEER_HALF ** -0.5),
        "peer_u": nrm(ks[12], (DEPTH, PEER_N_EXPERTS, D), D ** -0.5),
        "peer_v": nrm(ks[13], (DEPTH, PEER_N_EXPERTS, D), 0.2),
    }


def reference(x, meta, norm_mix, norm_ffn, sb_w_qkv, sb_q_gain, sb_k_gain, sb_w_o,
              pool_w, pool_scale, peer_w_q, peer_subkeys, peer_u, peer_v):
    B = x.shape[0]
    h = jnp.concatenate([jnp.broadcast_to(meta[None].astype(x.dtype), (B, N_META, x.shape[-1])), x], axis=1)
    for i in range(DEPTH):
        j = i // N_MIXERS
        hn = rms_norm(h, norm_mix[i])
        if i % N_MIXERS == 0:
            h = h + stick_breaking_attention(hn, sb_w_qkv[j], sb_q_gain[j], sb_k_gain[j], sb_w_o[j])
        else:
            h = h + pool_mixer(hn, pool_w[j], pool_scale[j])
        hn = rms_norm(h, norm_ffn[i])
        h = h + peer_ffn(hn, peer_w_q[i], peer_subkeys[i], peer_u[i], peer_v[i])
    return h[:, N_META:]
```

```python
import functools
import math

import jax
import jax.numpy as jnp
from jax import lax
from jax.experimental import pallas as pl
from jax.experimental.pallas import tpu as pltpu

F32 = jnp.float32
MXU_DTYPE = jnp.bfloat16

RMS_EPS = 1e-6
N_META = 16
SB_HEAD_DIM = 64
POOL_WINDOWS = (2, 4, 8, 16)
POOL_GROUP = 256
PEER_HEADS = 8
PEER_N_KEYS = 128
PEER_TOPK = 16

LANES = 128
ATTN_TILE = 256
ROW_TILE = 512
PEER_TOKENS = 512
PEER_EXPERTS = 512
POOL_HALO = 16
VMEM_LIMIT = 56 * 1024 * 1024


def _rms(x, gain):
    ms = jnp.mean(x * x, axis=-1, keepdims=True)
    return x * lax.rsqrt(ms + RMS_EPS) * gain


def _params(*sem):
    return pltpu.CompilerParams(dimension_semantics=sem, vmem_limit_bytes=VMEM_LIMIT)


def _qkv_kernel(h_ref, g_ref, w_ref, bd_ref, qg_ref, kg_ref, q_ref, k_ref, v_ref):
    d = h_ref.shape[1]
    hn = _rms(h_ref[...], g_ref[...]).astype(MXU_DTYPE)
    bd = bd_ref[...]
    for src, gref, dst in ((0, qg_ref, q_ref), (1, kg_ref, k_ref)):
        t = jnp.dot(hn, w_ref[:, src * d:(src + 1) * d], preferred_element_type=F32)
        for c in range(d // LANES):
            sl = slice(c * LANES, (c + 1) * LANES)
            tc = t[:, sl]
            ms = jnp.dot((tc * tc).astype(MXU_DTYPE), bd, preferred_element_type=F32)
            dst[:, sl] = (tc * lax.rsqrt(ms + RMS_EPS) * gref[:, sl]).astype(dst.dtype)
    v_ref[...] = jnp.dot(hn, w_ref[:, 2 * d:], preferred_element_type=F32).astype(v_ref.dtype)


def _qkv_call(h2d, gain, w_qkv, q_gain, k_gain):
    t, d = h2d.shape
    scale = 1.0 / math.sqrt(SB_HEAD_DIM)
    reps = d // SB_HEAD_DIM
    qg = jnp.tile(q_gain.astype(F32) * scale, reps)[None, :]
    kg = jnp.tile(k_gain.astype(F32), reps)[None, :]
    idx = jnp.arange(LANES) // SB_HEAD_DIM
    bd = (idx[:, None] == idx[None, :]).astype(F32) / SB_HEAD_DIM
    row = pl.BlockSpec((ROW_TILE, d), lambda i: (i, 0))
    full = lambda shape: pl.BlockSpec(shape, lambda i: (0,) * len(shape))
    out = jax.ShapeDtypeStruct((t, d), MXU_DTYPE)
    return pl.pallas_call(
        _qkv_kernel,
        grid=(t // ROW_TILE,),
        in_specs=[row, full((1, d)), full((d, 3 * d)), full((LANES, LANES)), full((1, d)), full((1, d))],
        out_specs=[row, row, row],
        out_shape=[out, out, out],
        compiler_params=_params("arbitrary"),
        name="qkv_proj",
    )(h2d, gain[None, :].astype(F32), w_qkv.astype(MXU_DTYPE), bd.astype(MXU_DTYPE), qg, kg)


def _softplus(z):
    return jnp.maximum(z, 0.0) + jnp.log(1.0 + jnp.exp(-jnp.abs(z)))


def _suffix_sums(sp, uo):
    if MXU_DTYPE == F32:
        return jnp.dot(sp, uo, preferred_element_type=F32)
    hi = sp.astype(MXU_DTYPE)
    lo = (sp - hi.astype(F32)).astype(MXU_DTYPE)
    return (jnp.dot(hi, uo, preferred_element_type=F32)
            + jnp.dot(lo, uo, preferred_element_type=F32))


def _attn_kernel(q_ref, k_ref, v_ref, uo_ref, o_ref, *, tile):
    qi = pl.program_id(2)
    q2 = q_ref[0]
    lane = lax.broadcasted_iota(jnp.int32, (1, LANES), 1)
    row = lax.broadcasted_iota(jnp.int32, (tile, tile), 0)
    col = lax.broadcasted_iota(jnp.int32, (tile, tile), 1)
    causal = col < row
    uo = uo_ref[...]

    def step(qm, kj, acc, later, mask):
        start = pl.multiple_of(kj * tile, tile)
        kt = k_ref[0, pl.ds(start, tile), :]
        vt = v_ref[0, pl.ds(start, tile), :]
        z = lax.dot_general(qm, kt, (((1,), (1,)), ((), ())), preferred_element_type=F32)
        sp = _softplus(z)
        if mask is not None:
            sp = jnp.where(mask, sp, 0.0)
        r = _suffix_sums(sp, uo)
        after = r[:, :tile]
        total = r[:, tile:]
        later_w = jnp.concatenate([later] * (tile // LANES), axis=1)
        a = jnp.exp(z - sp - after - later_w)
        if mask is not None:
            a = jnp.where(mask, a, 0.0)
        acc = acc + jnp.dot(a.astype(vt.dtype), vt, preferred_element_type=F32)
        return acc, later + total

    heads = []
    for hh in range(2):
        in_head = (lane >= hh * SB_HEAD_DIM) & (lane < (hh + 1) * SB_HEAD_DIM)
        qm = jnp.where(in_head, q2, jnp.zeros_like(q2))
        zero = jnp.zeros((tile, LANES), F32)
        acc, later = step(qm, qi, zero, zero, causal)
        acc, later = lax.fori_loop(
            0, qi, lambda j, c, qm=qm: step(qm, qi - 1 - j, c[0], c[1], None), (acc, later))
        heads.append(acc)
    o_ref[0] = jnp.where(lane < SB_HEAD_DIM, heads[0], heads[1]).astype(o_ref.dtype)


def _attn_call(q, k, v):
    b, lp, d = q.shape
    tile = ATTN_TILE
    j = jnp.arange(tile)
    strict = (j[:, None] > j[None, :]).astype(F32)
    uo = jnp.concatenate([strict, jnp.ones((tile, LANES), F32)], axis=1).astype(MXU_DTYPE)
    qspec = pl.BlockSpec((1, tile, LANES), lambda bi, hp, qi: (bi, qi, hp))
    kvspec = pl.BlockSpec((1, lp, LANES), lambda bi, hp, qi: (bi, 0, hp))
    return pl.pallas_call(
        functools.partial(_attn_kernel, tile=tile),
        grid=(b, d // LANES, lp // tile),
        in_specs=[qspec, kvspec, kvspec, pl.BlockSpec(uo.shape, lambda bi, hp, qi: (0, 0))],
        out_specs=qspec,
        out_shape=jax.ShapeDtypeStruct((b, lp, d), MXU_DTYPE),
        compiler_params=_params("arbitrary", "arbitrary", "arbitrary"),
        name="sb_attention",
    )(q, k, v, uo)


def _proj_kernel(h_ref, o_ref, w_ref, out_ref):
    out_ref[...] = h_ref[...] + jnp.dot(o_ref[...], w_ref[...], preferred_element_type=F32)


def _proj_call(h2d, o2d, w_o):
    t, d = h2d.shape
    row = pl.BlockSpec((ROW_TILE, d), lambda i: (i, 0))
    return pl.pallas_call(
        _proj_kernel,
        grid=(t // ROW_TILE,),
        in_specs=[row, row, pl.BlockSpec((d, d), lambda i: (0, 0))],
        out_specs=row,
        out_shape=jax.ShapeDtypeStruct((t, d), F32),
        compiler_params=_params("arbitrary"),
        name="attn_out_proj",
    )(h2d, o2d, w_o.astype(MXU_DTYPE))


def _pool_kernel(h_ref, halo_ref, g_ref, w_ref, s_ref, out_ref, ext_ref, *, ts):
    i = pl.program_id(1)
    x = h_ref[0]
    gain = g_ref[...]
    hn = _rms(x, gain)
    halo = jnp.where(i > 0, _rms(halo_ref[0], gain), 0.0)
    ext_ref[0:POOL_HALO, :] = halo
    ext_ref[POOL_HALO:, :] = hn
    pos = i * ts + lax.broadcasted_iota(jnp.int32, (ts, 1), 0)
    for g, w in enumerate(POOL_WINDOWS):
        sl = slice(g * POOL_GROUP, (g + 1) * POOL_GROUP)
        acc = ext_ref[POOL_HALO:, sl]
        for kk in range(1, w):
            acc = acc + ext_ref[POOL_HALO - kk:POOL_HALO - kk + ts, sl]
        cnt = jnp.minimum(pos + 1, w).astype(F32)
        pooled = acc / cnt - ext_ref[POOL_HALO:, sl]
        y = jnp.dot(pooled.astype(MXU_DTYPE), w_ref[g], preferred_element_type=F32)
        out_ref[0, :, sl] = x[:, sl] + y * s_ref[:, sl]


def _pool_call(h, gain, pool_w, pool_scale):
    b, lp, d = h.shape
    ts = ATTN_TILE
    per = ts // POOL_HALO
    ng = len(POOL_WINDOWS)
    return pl.pallas_call(
        functools.partial(_pool_kernel, ts=ts),
        grid=(b, lp // ts),
        in_specs=[
            pl.BlockSpec((1, ts, d), lambda bi, i: (bi, i, 0)),
            pl.BlockSpec((1, POOL_HALO, d), lambda bi, i: (bi, jnp.maximum(i * per - 1, 0), 0)),
            pl.BlockSpec((1, d), lambda bi, i: (0, 0)),
            pl.BlockSpec((ng, POOL_GROUP, POOL_GROUP), lambda bi, i: (0, 0, 0)),
            pl.BlockSpec((1, d), lambda bi, i: (0, 0)),
        ],
        out_specs=pl.BlockSpec((1, ts, d), lambda bi, i: (bi, i, 0)),
        out_shape=jax.ShapeDtypeStruct((b, lp, d), F32),
        scratch_shapes=[pltpu.VMEM((POOL_HALO + ts, d), F32)],
        compiler_params=_params("arbitrary", "arbitrary"),
        name="pool_mixer",
    )(h, h, gain[None, :].astype(F32), pool_w.astype(MXU_DTYPE), pool_scale[None, :].astype(F32))


_PEER_RANKS = PEER_TOPK + 1
_PEER_PAIRS = tuple((a, b) for a in range(_PEER_RANKS) for b in range(_PEER_RANKS)
                    if (a + 1) * (b + 1) <= _PEER_RANKS)
_PEER_CAND_ROWS = -(-len(_PEER_PAIRS) // 8) * 8
_PEER_RANK_ROWS = -(-_PEER_RANKS // 8) * 8


def _extract_maxima(cur_ref, dst_ref, n):
    def body(r, carry):
        cur = cur_ref[...]
        m = jnp.max(cur, axis=0, keepdims=True)
        dst_ref[pl.ds(r, 1), :] = m
        cur_ref[...] = jnp.where(cur == m, -jnp.inf, cur)
        return carry
    lax.fori_loop(0, n, body, 0)


def _peer_kernel(h_ref, g_ref, wq_ref, sk_ref, u_ref, vt_ref, out_ref,
                 hn_ref, s2_ref, e2_ref, thr_ref, e1_ref, acc_ref,
                 cur_ref, t1_ref, t2_ref, cand_ref, top_ref, *, n_sub):
    e = pl.program_id(1)
    nk = PEER_N_KEYS

    @pl.when(e == 0)
    def _prepare():
        hn = _rms(h_ref[...], g_ref[...]).astype(MXU_DTYPE)
        hn_ref[...] = hn
        acc_ref[...] = jnp.zeros_like(acc_ref)

        def head(hd, carry):
            wq = wq_ref[pl.ds(pl.multiple_of(hd * 2 * nk, 2 * nk), 2 * nk), :]
            qt = lax.dot_general(wq, hn, (((1,), (1,)), ((), ())), preferred_element_type=F32)
            s1 = jnp.dot(sk_ref[0], qt[:nk].astype(MXU_DTYPE), preferred_element_type=F32)
            s2 = jnp.dot(sk_ref[1], qt[nk:].astype(MXU_DTYPE), preferred_element_type=F32)
            cur_ref[...] = s1
            _extract_maxima(cur_ref, t1_ref, _PEER_RANKS)
            cur_ref[...] = s2
            _extract_maxima(cur_ref, t2_ref, _PEER_RANKS)
            cand_ref[...] = jnp.full(cand_ref.shape, -jnp.inf, F32)
            for idx, (a, b) in enumerate(_PEER_PAIRS):
                cand_ref[idx:idx + 1, :] = t1_ref[a:a + 1, :] + t2_ref[b:b + 1, :]
            _extract_maxima(cand_ref, top_ref, _PEER_RANKS)
            top = top_ref[...]
            best = top[0:1, :]
            z = jnp.sum(jnp.exp(top[0:PEER_TOPK, :] - best), axis=0, keepdims=True)
            tau = 0.5 * (top[PEER_TOPK - 1:PEER_TOPK, :] + top[PEER_TOPK:PEER_TOPK + 1, :])
            thr_ref[hd] = tau - s1
            s2_ref[hd] = s2
            e1_ref[hd] = jnp.exp(s1 - t1_ref[0:1, :])
            e2_ref[hd] = jnp.exp(s2 - t2_ref[0:1, :]) / z
            return carry

        lax.fori_loop(0, PEER_HEADS, head, 0)

    pre = lax.dot_general(u_ref[...], hn_ref[...], (((1,), (1,)), ((), ())),
                          preferred_element_type=F32)
    act = 0.5 * pre * (1.0 + lax.erf(pre * math.sqrt(0.5)))
    parts = []
    for ii in range(n_sub):
        i = e * n_sub + ii
        w = None
        for hd in range(PEER_HEADS):
            sel = s2_ref[hd] >= thr_ref[hd, pl.ds(i, 1), :]
            term = jnp.where(sel, e2_ref[hd] * e1_ref[hd, pl.ds(i, 1), :], 0.0)
            w = term if w is None else w + term
        parts.append((w * act[ii * nk:(ii + 1) * nk, :]).astype(MXU_DTYPE))
    p = jnp.concatenate(parts, axis=0)
    acc_ref[...] += jnp.dot(vt_ref[...], p, preferred_element_type=F32)

    @pl.when(e == pl.num_programs(1) - 1)
    def _finish():
        out_ref[...] = h_ref[...] + acc_ref[...].T


def _peer_call(h2d, gain, w_q, subkeys, u_tab, v_tab):
    t, d = h2d.shape
    n_exp = u_tab.shape[0]
    c, te = PEER_TOKENS, PEER_EXPERTS
    nk = PEER_N_KEYS
    tab = lambda: pltpu.VMEM((PEER_HEADS, nk, c), F32)
    return pl.pallas_call(
        functools.partial(_peer_kernel, n_sub=te // nk),
        grid=(t // c, n_exp // te),
        in_specs=[
            pl.BlockSpec((c, d), lambda ti, e: (ti, 0)),
            pl.BlockSpec((1, d), lambda ti, e: (0, 0)),
            pl.BlockSpec((PEER_HEADS * 2 * nk, d), lambda ti, e: (0, 0)),
            pl.BlockSpec((2, nk, nk), lambda ti, e: (0, 0, 0)),
            pl.BlockSpec((te, d), lambda ti, e: (e, 0)),
            pl.BlockSpec((d, te), lambda ti, e: (0, e)),
        ],
        out_specs=pl.BlockSpec((c, d), lambda ti, e: (ti, 0)),
        out_shape=jax.ShapeDtypeStruct((t, d), F32),
        scratch_shapes=[
            pltpu.VMEM((c, d), MXU_DTYPE),
            tab(), tab(), tab(), tab(),
            pltpu.VMEM((d, c), F32),
            pltpu.VMEM((nk, c), F32),
            pltpu.VMEM((_PEER_RANK_ROWS, c), F32),
            pltpu.VMEM((_PEER_RANK_ROWS, c), F32),
            pltpu.VMEM((_PEER_CAND_ROWS, c), F32),
            pltpu.VMEM((_PEER_RANK_ROWS, c), F32),
        ],
        compiler_params=_params("arbitrary", "arbitrary"),
        name="peer_ffn",
    )(h2d, gain[None, :].astype(F32), w_q.T.astype(MXU_DTYPE), subkeys.astype(MXU_DTYPE),
      u_tab.astype(MXU_DTYPE), v_tab.T.astype(MXU_DTYPE))


def kernel(x, meta, norm_mix, norm_ffn, sb_w_qkv, sb_q_gain, sb_k_gain, sb_w_o, pool_w, pool_scale,
           peer_w_q, peer_subkeys, peer_u, peer_v):
    b, seq, d = x.shape
    depth = norm_mix.shape[0]
    l = N_META + seq
    lp = -(-l // ATTN_TILE) * ATTN_TILE
    while (b * lp) % math.lcm(ROW_TILE, PEER_TOKENS):
        lp += ATTN_TILE
    h = jnp.concatenate([jnp.broadcast_to(meta[None].astype(x.dtype), (b, N_META, d)), x,
                         jnp.zeros((b, lp - l, d), x.dtype)], axis=1)
    for i in range(depth):
        j = i // 2
        if i % 2 == 0:
            h2d = h.reshape(b * lp, d)
            q, k, v = _qkv_call(h2d, norm_mix[i], sb_w_qkv[j], sb_q_gain[j], sb_k_gain[j])
            o = _attn_call(q.reshape(b, lp, d), k.reshape(b, lp, d), v.reshape(b, lp, d))
            h2d = _proj_call(h2d, o.reshape(b * lp, d), sb_w_o[j])
        else:
            h2d = _pool_call(h, norm_mix[i], pool_w[j], pool_scale[j]).reshape(b * lp, d)
        h2d = _peer_call(h2d, norm_ffn[i], peer_w_q[i], peer_subkeys[i], peer_u[i], peer_v[i])
        h = h2d.reshape(b, lp, d)
    return h[:, N_META:l]
```

```python
import functools
import math

import jax
import jax.numpy as jnp
from jax import lax
from jax.experimental import pallas as pl
from jax.experimental.pallas import tpu as pltpu

F32 = jnp.float32
MXU_DTYPE = jnp.bfloat16

RMS_EPS = 1e-6
N_META = 16
SB_HEAD_DIM = 64
POOL_WINDOWS = (2, 4, 8, 16)
POOL_GROUP = 256
PEER_HEADS = 8
PEER_N_KEYS = 128
PEER_TOPK = 16

LANES = 128
SUBLANES = 8
ATTN_TILE = 256
ROW_TILE = 512
PEER_TOKENS = 512
PEER_EXPERTS = SUBLANES * PEER_N_KEYS
POOL_HALO = 16
VMEM_LIMIT = 56 * 1024 * 1024


def _rms(x, gain):
    ms = jnp.mean(x * x, axis=-1, keepdims=True)
    return x * lax.rsqrt(ms + RMS_EPS) * gain


def _params(*sem, flags=None):
    return pltpu.CompilerParams(dimension_semantics=sem, vmem_limit_bytes=VMEM_LIMIT, flags=flags)


def _qkv_kernel(h_ref, g_ref, w_ref, bd_ref, qg_ref, kg_ref, q_ref, k_ref, v_ref):
    d = h_ref.shape[1]
    hn = _rms(h_ref[...], g_ref[...]).astype(MXU_DTYPE)
    bd = bd_ref[...]
    for src, gref, dst in ((0, qg_ref, q_ref), (1, kg_ref, k_ref)):
        t = jnp.dot(hn, w_ref[:, src * d:(src + 1) * d], preferred_element_type=F32)
        for c in range(d // LANES):
            sl = slice(c * LANES, (c + 1) * LANES)
            tc = t[:, sl]
            ms = jnp.dot((tc * tc).astype(MXU_DTYPE), bd, preferred_element_type=F32)
            dst[:, sl] = (tc * lax.rsqrt(ms + RMS_EPS) * gref[:, sl]).astype(dst.dtype)
    v_ref[...] = jnp.dot(hn, w_ref[:, 2 * d:], preferred_element_type=F32).astype(v_ref.dtype)


def _qkv_call(h2d, gain, w_qkv, q_gain, k_gain):
    t, d = h2d.shape
    scale = LOG2E / math.sqrt(SB_HEAD_DIM)
    reps = d // SB_HEAD_DIM
    qg = jnp.tile(q_gain.astype(F32) * scale, reps)[None, :]
    kg = jnp.tile(k_gain.astype(F32), reps)[None, :]
    idx = jnp.arange(LANES) // SB_HEAD_DIM
    bd = (idx[:, None] == idx[None, :]).astype(F32) / SB_HEAD_DIM
    row = pl.BlockSpec((ROW_TILE, d), lambda i: (i, 0))
    full = lambda shape: pl.BlockSpec(shape, lambda i: (0,) * len(shape))
    out = jax.ShapeDtypeStruct((t, d), MXU_DTYPE)
    return pl.pallas_call(
        _qkv_kernel,
        grid=(t // ROW_TILE,),
        in_specs=[row, full((1, d)), full((d, 3 * d)), full((LANES, LANES)), full((1, d)), full((1, d))],
        out_specs=[row, row, row],
        out_shape=[out, out, out],
        compiler_params=_params("arbitrary"),
        name="qkv_proj",
    )(h2d, gain[None, :].astype(F32), w_qkv.astype(MXU_DTYPE), bd.astype(MXU_DTYPE), qg, kg)


LOG2E = math.log2(math.e)
ATTN_EXIT = 160.0


def _softplus2(z):
    return jnp.maximum(z, 0.0) + jnp.log(1.0 + jnp.exp2(-jnp.abs(z))) * LOG2E


def _suffix_sums(sp, uo):
    if MXU_DTYPE == F32:
        return jnp.dot(sp, uo, preferred_element_type=F32)
    hi = sp.astype(MXU_DTYPE)
    lo = (sp - hi.astype(F32)).astype(MXU_DTYPE)
    return (jnp.dot(hi, uo, preferred_element_type=F32)
            + jnp.dot(lo, uo, preferred_element_type=F32))


def _attn_kernel(q_ref, k_ref, v_ref, uo_ref, o_ref, acc_ref, later_ref, *, tile):
    qi = pl.program_id(2)
    q2 = q_ref[0]
    lane = lax.broadcasted_iota(jnp.int32, (1, LANES), 1)
    row = lax.broadcasted_iota(jnp.int32, (tile, tile), 0)
    col = lax.broadcasted_iota(jnp.int32, (tile, tile), 1)
    causal = col < row
    uo = uo_ref[...]
    qms = [jnp.where((lane >= hh * SB_HEAD_DIM) & (lane < (hh + 1) * SB_HEAD_DIM), q2,
                     jnp.zeros_like(q2)) for hh in range(2)]

    def step(kj, diagonal):
        start = pl.multiple_of(kj * tile, tile)
        kt = k_ref[0, pl.ds(start, tile), :]
        vt = v_ref[0, pl.ds(start, tile), :]
        for hh in range(2):
            z = lax.dot_general(qms[hh], kt, (((1,), (1,)), ((), ())), preferred_element_type=F32)
            sp = _softplus2(z)
            if diagonal:
                sp = jnp.where(causal, sp, 0.0)
            r = _suffix_sums(sp, uo)
            logit = z - sp - r[:, :tile]
            total = r[:, tile:]
            if diagonal:
                a = jnp.where(causal, jnp.exp2(logit), 0.0)
            else:
                later = later_ref[hh]
                a = jnp.exp2(logit - jnp.concatenate([later] * (tile // LANES), axis=1))
            pv = jnp.dot(a.astype(vt.dtype), vt, preferred_element_type=F32)
            if diagonal:
                acc_ref[hh] = pv
                later_ref[hh] = total
            else:
                acc_ref[hh] += pv
                later_ref[hh] = later + total

    step(qi, True)

    def more(state):
        j, live = state
        return (j < qi) & (live > 0)

    def body(state):
        j, _ = state
        step(qi - 1 - j, False)
        reach = jnp.min(jnp.minimum(later_ref[0], later_ref[1]))
        return j + 1, (reach < ATTN_EXIT).astype(jnp.int32)

    lax.while_loop(more, body, (jnp.int32(0), jnp.int32(1)))
    o_ref[0] = jnp.where(lane < SB_HEAD_DIM, acc_ref[0], acc_ref[1]).astype(o_ref.dtype)


def _attn_call(q, k, v):
    b, lp, d = q.shape
    tile = ATTN_TILE
    j = jnp.arange(tile)
    strict = (j[:, None] > j[None, :]).astype(F32)
    uo = jnp.concatenate([strict, jnp.ones((tile, LANES), F32)], axis=1).astype(MXU_DTYPE)
    qspec = pl.BlockSpec((1, tile, LANES), lambda bi, hp, qi: (bi, qi, hp))
    kvspec = pl.BlockSpec((1, lp, LANES), lambda bi, hp, qi: (bi, 0, hp))
    return pl.pallas_call(
        functools.partial(_attn_kernel, tile=tile),
        grid=(b, d // LANES, lp // tile),
        in_specs=[qspec, kvspec, kvspec, pl.BlockSpec(uo.shape, lambda bi, hp, qi: (0, 0))],
        out_specs=qspec,
        out_shape=jax.ShapeDtypeStruct((b, lp, d), MXU_DTYPE),
        scratch_shapes=[pltpu.VMEM((2, tile, LANES), F32),
                        pltpu.VMEM((2, tile, LANES), F32)],
        compiler_params=_params("arbitrary", "arbitrary", "arbitrary"),
        name="sb_attention",
    )(q, k, v, uo)


def _proj_kernel(h_ref, o_ref, w_ref, out_ref):
    out_ref[...] = h_ref[...] + jnp.dot(o_ref[...], w_ref[...], preferred_element_type=F32)


def _proj_call(h2d, o2d, w_o):
    t, d = h2d.shape
    row = pl.BlockSpec((ROW_TILE, d), lambda i: (i, 0))
    return pl.pallas_call(
        _proj_kernel,
        grid=(t // ROW_TILE,),
        in_specs=[row, row, pl.BlockSpec((d, d), lambda i: (0, 0))],
        out_specs=row,
        out_shape=jax.ShapeDtypeStruct((t, d), F32),
        compiler_params=_params("arbitrary"),
        name="attn_out_proj",
    )(h2d, o2d, w_o.astype(MXU_DTYPE))


def _pool_kernel(h_ref, halo_ref, g_ref, w_ref, s_ref, out_ref, ext_ref, *, ts):
    i = pl.program_id(1)
    x = h_ref[0]
    gain = g_ref[...]
    hn = _rms(x, gain)
    halo = jnp.where(i > 0, _rms(halo_ref[0], gain), 0.0)
    ext_ref[0:POOL_HALO, :] = halo
    ext_ref[POOL_HALO:, :] = hn
    pos = i * ts + lax.broadcasted_iota(jnp.int32, (ts, 1), 0)
    for g, w in enumerate(POOL_WINDOWS):
        sl = slice(g * POOL_GROUP, (g + 1) * POOL_GROUP)
        acc = ext_ref[POOL_HALO:, sl]
        for kk in range(1, w):
            acc = acc + ext_ref[POOL_HALO - kk:POOL_HALO - kk + ts, sl]
        cnt = jnp.minimum(pos + 1, w).astype(F32)
        pooled = acc / cnt - ext_ref[POOL_HALO:, sl]
        y = jnp.dot(pooled.astype(MXU_DTYPE), w_ref[g], preferred_element_type=F32)
        out_ref[0, :, sl] = x[:, sl] + y * s_ref[:, sl]


def _pool_call(h, gain, pool_w, pool_scale):
    b, lp, d = h.shape
    ts = ATTN_TILE
    per = ts // POOL_HALO
    ng = len(POOL_WINDOWS)
    return pl.pallas_call(
        functools.partial(_pool_kernel, ts=ts),
        grid=(b, lp // ts),
        in_specs=[
            pl.BlockSpec((1, ts, d), lambda bi, i: (bi, i, 0)),
            pl.BlockSpec((1, POOL_HALO, d), lambda bi, i: (bi, jnp.maximum(i * per - 1, 0), 0)),
            pl.BlockSpec((1, d), lambda bi, i: (0, 0)),
            pl.BlockSpec((ng, POOL_GROUP, POOL_GROUP), lambda bi, i: (0, 0, 0)),
            pl.BlockSpec((1, d), lambda bi, i: (0, 0)),
        ],
        out_specs=pl.BlockSpec((1, ts, d), lambda bi, i: (bi, i, 0)),
        out_shape=jax.ShapeDtypeStruct((b, lp, d), F32),
        scratch_shapes=[pltpu.VMEM((POOL_HALO + ts, d), F32)],
        compiler_params=_params("arbitrary", "arbitrary"),
        name="pool_mixer",
    )(h, h, gain[None, :].astype(F32), pool_w.astype(MXU_DTYPE), pool_scale[None, :].astype(F32))


_PEER_RANKS = PEER_TOPK + 1
_PEER_PAIRS = tuple((a, b) for a in range(_PEER_RANKS) for b in range(_PEER_RANKS)
                    if (a + 1) * (b + 1) <= _PEER_RANKS)
_PEER_CAND_ROWS = -(-len(_PEER_PAIRS) // 8) * 8
_PEER_RANK_ROWS = -(-_PEER_RANKS // 8) * 8


def _extract_maxima(cur_ref, dst_ref, n):
    def body(r, carry):
        cur = cur_ref[...]
        m = jnp.max(cur, axis=0, keepdims=True)
        dst_ref[pl.ds(r, 1), :] = m
        cur_ref[...] = jnp.where(cur == m, -jnp.inf, cur)
        return carry
    lax.fori_loop(0, n, body, 0)


def _peer_kernel(h_ref, g_ref, wq_ref, sk_ref, u_ref, vt_ref, out_ref,
                 hn_ref, e2_ref, e1_ref, phi_ref, acc_ref, p_ref, pre_ref,
                 cur_ref, t1_ref, t2_ref, cand_ref, top_ref):
    e = pl.program_id(1)
    nk = PEER_N_KEYS

    @pl.when(e == 0)
    def _prepare():
        hn = _rms(h_ref[...], g_ref[...]).astype(MXU_DTYPE)
        hn_ref[...] = hn
        acc_ref[...] = jnp.zeros_like(acc_ref)

        def head(hd, carry):
            wq = wq_ref[pl.ds(pl.multiple_of(hd * 2 * nk, 2 * nk), 2 * nk), :]
            qt = lax.dot_general(wq, hn, (((1,), (1,)), ((), ())), preferred_element_type=F32)
            s1 = jnp.dot(sk_ref[0], qt[:nk].astype(MXU_DTYPE), preferred_element_type=F32)
            s2 = jnp.dot(sk_ref[1], qt[nk:].astype(MXU_DTYPE), preferred_element_type=F32)
            cur_ref[...] = s1
            _extract_maxima(cur_ref, t1_ref, _PEER_RANKS)
            cur_ref[...] = s2
            _extract_maxima(cur_ref, t2_ref, _PEER_RANKS)
            cand_ref[...] = jnp.full(cand_ref.shape, -jnp.inf, F32)
            for idx, (a, b) in enumerate(_PEER_PAIRS):
                cand_ref[idx:idx + 1, :] = t1_ref[a:a + 1, :] + t2_ref[b:b + 1, :]
            _extract_maxima(cand_ref, top_ref, _PEER_RANKS)
            top = top_ref[...]
            best = top[0:1, :]
            z = jnp.sum(jnp.exp(top[0:PEER_TOPK, :] - best), axis=0, keepdims=True)
            tau = 0.5 * (top[PEER_TOPK - 1:PEER_TOPK, :] + top[PEER_TOPK:PEER_TOPK + 1, :])
            phi_ref[pl.ds(hd, 1), :] = jnp.exp(tau - best) / z
            e1 = jnp.exp(s1 - t1_ref[0:1, :])
            for blk in range(nk // SUBLANES):
                e1_ref[hd, blk] = e1[blk * SUBLANES:(blk + 1) * SUBLANES, :]
            e2_ref[hd] = (jnp.exp(s2 - t2_ref[0:1, :]) / z).astype(e2_ref.dtype)
            return carry

        lax.fori_loop(0, PEER_HEADS, head, 0)

    n_tok = hn_ref.shape[0]

    slab = 2 * nk
    n_slabs = u_ref.shape[0] // slab
    rows_per = 2 * SUBLANES

    def score(s):
        pre_ref[s] = lax.dot_general(u_ref[s * slab:(s + 1) * slab, :], hn_ref[...],
                                     (((1,), (1,)), ((), ())), preferred_element_type=F32)

    score(0)
    for s in range(n_slabs):
        if s + 1 < n_slabs:
            score(s + 1)
        for half in range(slab // nk):
            ii = s * (slab // nk) + half
            for cs in range(n_tok // LANES):
                lanes = slice(cs * LANES, (cs + 1) * LANES)
                e1_rows = [jnp.broadcast_to(e1_ref[hd, e, ii:ii + 1, lanes], (rows_per, LANES))
                           .astype(e2_ref.dtype) for hd in range(PEER_HEADS)]
                phi_rows = [jnp.broadcast_to(phi_ref[hd:hd + 1, lanes], (rows_per, LANES))
                            .astype(e2_ref.dtype) for hd in range(PEER_HEADS)]
                for jp in range(nk // rows_per):
                    w = None
                    for hd in range(PEER_HEADS):
                        g = e2_ref[hd, jp * rows_per:(jp + 1) * rows_per, lanes] * e1_rows[hd]
                        term = jnp.where(g >= phi_rows[hd], g, jnp.zeros_like(g))
                        w = term if w is None else w + term
                    rows = slice(half * nk + jp * rows_per, half * nk + (jp + 1) * rows_per)
                    pre = pre_ref[s, rows, lanes]
                    act = 0.5 * pre * (1.0 + lax.erf(pre * math.sqrt(0.5)))
                    p_ref[s, rows, lanes] = (w * act.astype(w.dtype)).astype(p_ref.dtype)
        acc_ref[...] += jnp.dot(vt_ref[:, s * slab:(s + 1) * slab], p_ref[s],
                                preferred_element_type=F32)

    @pl.when(e == pl.num_programs(1) - 1)
    def _finish():
        out_ref[...] = h_ref[...] + acc_ref[...].T


def _peer_call(h2d, gain, w_q, subkeys, u_tab, v_tab):
    t, d = h2d.shape
    n_exp = u_tab.shape[0]
    c, te = PEER_TOKENS, PEER_EXPERTS
    nk = PEER_N_KEYS
    slab = 2 * nk
    return pl.pallas_call(
        _peer_kernel,
        grid=(t // c, n_exp // te),
        in_specs=[
            pl.BlockSpec((c, d), lambda ti, e: (ti, 0)),
            pl.BlockSpec((1, d), lambda ti, e: (0, 0)),
            pl.BlockSpec((PEER_HEADS * 2 * nk, d), lambda ti, e: (0, 0)),
            pl.BlockSpec((2, nk, nk), lambda ti, e: (0, 0, 0)),
            pl.BlockSpec((te, d), lambda ti, e: (e, 0)),
            pl.BlockSpec((d, te), lambda ti, e: (0, e)),
        ],
        out_specs=pl.BlockSpec((c, d), lambda ti, e: (ti, 0)),
        out_shape=jax.ShapeDtypeStruct((t, d), F32),
        scratch_shapes=[
            pltpu.VMEM((c, d), MXU_DTYPE),
            pltpu.VMEM((PEER_HEADS, nk, c), MXU_DTYPE),
            pltpu.VMEM((PEER_HEADS, nk // SUBLANES, SUBLANES, c), F32),
            pltpu.VMEM((PEER_HEADS, c), F32),
            pltpu.VMEM((d, c), F32),
            pltpu.VMEM((te // slab, slab, c), MXU_DTYPE),
            pltpu.VMEM((te // slab, slab, c), F32),
            pltpu.VMEM((nk, c), F32),
            pltpu.VMEM((_PEER_RANK_ROWS, c), F32),
            pltpu.VMEM((_PEER_RANK_ROWS, c), F32),
            pltpu.VMEM((_PEER_CAND_ROWS, c), F32),
            pltpu.VMEM((_PEER_RANK_ROWS, c), F32),
        ],
        compiler_params=_params("arbitrary", "arbitrary"),
        name="peer_ffn",
    )(h2d, gain[None, :].astype(F32), w_q.T.astype(MXU_DTYPE), subkeys.astype(MXU_DTYPE),
      u_tab.astype(MXU_DTYPE), v_tab.T.astype(MXU_DTYPE))


def kernel(x, meta, norm_mix, norm_ffn, sb_w_qkv, sb_q_gain, sb_k_gain, sb_w_o, pool_w, pool_scale,
           peer_w_q, peer_subkeys, peer_u, peer_v):
    b, seq, d = x.shape
    depth = norm_mix.shape[0]
    l = N_META + seq
    lp = -(-l // ATTN_TILE) * ATTN_TILE
    while (b * lp) % math.lcm(ROW_TILE, PEER_TOKENS):
        lp += ATTN_TILE
    h = jnp.concatenate([jnp.broadcast_to(meta[None].astype(x.dtype), (b, N_META, d)), x,
                         jnp.zeros((b, lp - l, d), x.dtype)], axis=1)
    for i in range(depth):
        j = i // 2
        if i % 2 == 0:
            h2d = h.reshape(b * lp, d)
            q, k, v = _qkv_call(h2d, norm_mix[i], sb_w_qkv[j], sb_q_gain[j], sb_k_gain[j])
            o = _attn_call(q.reshape(b, lp, d), k.reshape(b, lp, d), v.reshape(b, lp, d))
            h2d = _proj_call(h2d, o.reshape(b * lp, d), sb_w_o[j])
        else:
            h2d = _pool_call(h, norm_mix[i], pool_w[j], pool_scale[j]).reshape(b * lp, d)
        h2d = _peer_call(h2d, norm_ffn[i], peer_w_q[i], peer_subkeys[i], peer_u[i], peer_v[i])
        h = h2d.reshape(b, lp, d)
    return h[:, N_META:l]
```

```python
import functools
import math

import jax
import jax.numpy as jnp
from jax import lax
from jax.experimental import pallas as pl
from jax.experimental.pallas import tpu as pltpu

F32 = jnp.float32
MXU_DTYPE = jnp.bfloat16

RMS_EPS = 1e-6
N_META = 16
SB_HEAD_DIM = 64
POOL_WINDOWS = (2, 4, 8, 16)
POOL_GROUP = 256
PEER_HEADS = 8
PEER_N_KEYS = 128
PEER_TOPK = 16

LANES = 128
SUBLANES = 8
ATTN_TILE = 256
ROW_TILE = 512
PEER_TOKENS = 1024
PEER_EXPERTS = SUBLANES * PEER_N_KEYS
POOL_HALO = 16
VMEM_LIMIT = 56 * 1024 * 1024


def _rms(x, gain):
    ms = jnp.mean(x * x, axis=-1, keepdims=True)
    return x * lax.rsqrt(ms + RMS_EPS) * gain


def _params(*sem, flags=None):
    return pltpu.CompilerParams(dimension_semantics=sem, vmem_limit_bytes=VMEM_LIMIT, flags=flags)


def _qkv_kernel(h_ref, g_ref, w_ref, bd_ref, qg_ref, kg_ref, q_ref, k_ref, v_ref):
    d = h_ref.shape[1]
    hn = _rms(h_ref[...], g_ref[...]).astype(MXU_DTYPE)
    bd = bd_ref[...]
    for src, gref, dst in ((0, qg_ref, q_ref), (1, kg_ref, k_ref)):
        t = jnp.dot(hn, w_ref[:, src * d:(src + 1) * d], preferred_element_type=F32)
        for c in range(d // LANES):
            sl = slice(c * LANES, (c + 1) * LANES)
            tc = t[:, sl]
            ms = jnp.dot((tc * tc).astype(MXU_DTYPE), bd, preferred_element_type=F32)
            dst[:, sl] = (tc * lax.rsqrt(ms + RMS_EPS) * gref[:, sl]).astype(dst.dtype)
    v_ref[...] = jnp.dot(hn, w_ref[:, 2 * d:], preferred_element_type=F32).astype(v_ref.dtype)


def _qkv_call(h2d, gain, w_qkv, q_gain, k_gain):
    t, d = h2d.shape
    scale = LOG2E / math.sqrt(SB_HEAD_DIM)
    reps = d // SB_HEAD_DIM
    qg = jnp.tile(q_gain.astype(F32) * scale, reps)[None, :]
    kg = jnp.tile(k_gain.astype(F32), reps)[None, :]
    idx = jnp.arange(LANES) // SB_HEAD_DIM
    bd = (idx[:, None] == idx[None, :]).astype(F32) / SB_HEAD_DIM
    row = pl.BlockSpec((ROW_TILE, d), lambda i: (i, 0))
    full = lambda shape: pl.BlockSpec(shape, lambda i: (0,) * len(shape))
    out = jax.ShapeDtypeStruct((t, d), MXU_DTYPE)
    return pl.pallas_call(
        _qkv_kernel,
        grid=(t // ROW_TILE,),
        in_specs=[row, full((1, d)), full((d, 3 * d)), full((LANES, LANES)), full((1, d)), full((1, d))],
        out_specs=[row, row, row],
        out_shape=[out, out, out],
        compiler_params=_params("arbitrary"),
        name="qkv_proj",
    )(h2d, gain[None, :].astype(F32), w_qkv.astype(MXU_DTYPE), bd.astype(MXU_DTYPE), qg, kg)


LOG2E = math.log2(math.e)
ATTN_EXIT = 160.0


def _softplus2(z):
    return jnp.maximum(z, 0.0) + jnp.log(1.0 + jnp.exp2(-jnp.abs(z))) * LOG2E


def _suffix_sums(sp, uo):
    if MXU_DTYPE == F32:
        return jnp.dot(sp, uo, preferred_element_type=F32)
    hi = sp.astype(MXU_DTYPE)
    lo = (sp - hi.astype(F32)).astype(MXU_DTYPE)
    return (jnp.dot(hi, uo, preferred_element_type=F32)
            + jnp.dot(lo, uo, preferred_element_type=F32))


def _attn_kernel(q_ref, k_ref, v_ref, uo_ref, o_ref, acc_ref, later_ref, *, tile):
    qi = pl.program_id(2)
    q2 = q_ref[0]
    lane = lax.broadcasted_iota(jnp.int32, (1, LANES), 1)
    row = lax.broadcasted_iota(jnp.int32, (tile, tile), 0)
    col = lax.broadcasted_iota(jnp.int32, (tile, tile), 1)
    causal = col < row
    uo = uo_ref[...]
    qms = [jnp.where((lane >= hh * SB_HEAD_DIM) & (lane < (hh + 1) * SB_HEAD_DIM), q2,
                     jnp.zeros_like(q2)) for hh in range(2)]

    def step(kj, diagonal):
        start = pl.multiple_of(kj * tile, tile)
        kt = k_ref[0, pl.ds(start, tile), :]
        vt = v_ref[0, pl.ds(start, tile), :]
        for hh in range(2):
            z = lax.dot_general(qms[hh], kt, (((1,), (1,)), ((), ())), preferred_element_type=F32)
            sp = _softplus2(z)
            if diagonal:
                sp = jnp.where(causal, sp, 0.0)
            r = _suffix_sums(sp, uo)
            logit = z - sp - r[:, :tile]
            total = r[:, tile:]
            if diagonal:
                a = jnp.where(causal, jnp.exp2(logit), 0.0)
            else:
                later = later_ref[hh]
                a = jnp.exp2(logit - jnp.concatenate([later] * (tile // LANES), axis=1))
            pv = jnp.dot(a.astype(vt.dtype), vt, preferred_element_type=F32)
            if diagonal:
                acc_ref[hh] = pv
                later_ref[hh] = total
            else:
                acc_ref[hh] += pv
                later_ref[hh] = later + total

    step(qi, True)

    def more(state):
        j, live = state
        return (j < qi) & (live > 0)

    def body(state):
        j, _ = state
        step(qi - 1 - j, False)
        reach = jnp.min(jnp.minimum(later_ref[0], later_ref[1]))
        return j + 1, (reach < ATTN_EXIT).astype(jnp.int32)

    lax.while_loop(more, body, (jnp.int32(0), jnp.int32(1)))
    o_ref[0] = jnp.where(lane < SB_HEAD_DIM, acc_ref[0], acc_ref[1]).astype(o_ref.dtype)


def _attn_call(q, k, v):
    b, lp, d = q.shape
    tile = ATTN_TILE
    j = jnp.arange(tile)
    strict = (j[:, None] > j[None, :]).astype(F32)
    uo = jnp.concatenate([strict, jnp.ones((tile, LANES), F32)], axis=1).astype(MXU_DTYPE)
    qspec = pl.BlockSpec((1, tile, LANES), lambda bi, hp, qi: (bi, qi, hp))
    kvspec = pl.BlockSpec((1, lp, LANES), lambda bi, hp, qi: (bi, 0, hp))
    return pl.pallas_call(
        functools.partial(_attn_kernel, tile=tile),
        grid=(b, d // LANES, lp // tile),
        in_specs=[qspec, kvspec, kvspec, pl.BlockSpec(uo.shape, lambda bi, hp, qi: (0, 0))],
        out_specs=qspec,
        out_shape=jax.ShapeDtypeStruct((b, lp, d), MXU_DTYPE),
        scratch_shapes=[pltpu.VMEM((2, tile, LANES), F32),
                        pltpu.VMEM((2, tile, LANES), F32)],
        compiler_params=_params("arbitrary", "arbitrary", "arbitrary"),
        name="sb_attention",
    )(q, k, v, uo)


def _proj_kernel(h_ref, o_ref, w_ref, out_ref):
    out_ref[...] = h_ref[...] + jnp.dot(o_ref[...], w_ref[...], preferred_element_type=F32)


def _proj_call(h2d, o2d, w_o):
    t, d = h2d.shape
    row = pl.BlockSpec((ROW_TILE, d), lambda i: (i, 0))
    return pl.pallas_call(
        _proj_kernel,
        grid=(t // ROW_TILE,),
        in_specs=[row, row, pl.BlockSpec((d, d), lambda i: (0, 0))],
        out_specs=row,
        out_shape=jax.ShapeDtypeStruct((t, d), F32),
        compiler_params=_params("arbitrary"),
        name="attn_out_proj",
    )(h2d, o2d, w_o.astype(MXU_DTYPE))


def _pool_kernel(h_ref, halo_ref, g_ref, w_ref, s_ref, out_ref, ext_ref, *, ts):
    i = pl.program_id(1)
    x = h_ref[0]
    gain = g_ref[...]
    hn = _rms(x, gain)
    halo = jnp.where(i > 0, _rms(halo_ref[0], gain), 0.0)
    ext_ref[0:POOL_HALO, :] = halo
    ext_ref[POOL_HALO:, :] = hn
    pos = i * ts + lax.broadcasted_iota(jnp.int32, (ts, 1), 0)
    for g, w in enumerate(POOL_WINDOWS):
        sl = slice(g * POOL_GROUP, (g + 1) * POOL_GROUP)
        acc = ext_ref[POOL_HALO:, sl]
        for kk in range(1, w):
            acc = acc + ext_ref[POOL_HALO - kk:POOL_HALO - kk + ts, sl]
        cnt = jnp.minimum(pos + 1, w).astype(F32)
        pooled = acc / cnt - ext_ref[POOL_HALO:, sl]
        y = jnp.dot(pooled.astype(MXU_DTYPE), w_ref[g], preferred_element_type=F32)
        out_ref[0, :, sl] = x[:, sl] + y * s_ref[:, sl]


def _pool_call(h, gain, pool_w, pool_scale):
    b, lp, d = h.shape
    ts = ATTN_TILE
    per = ts // POOL_HALO
    ng = len(POOL_WINDOWS)
    return pl.pallas_call(
        functools.partial(_pool_kernel, ts=ts),
        grid=(b, lp // ts),
        in_specs=[
            pl.BlockSpec((1, ts, d), lambda bi, i: (bi, i, 0)),
            pl.BlockSpec((1, POOL_HALO, d), lambda bi, i: (bi, jnp.maximum(i * per - 1, 0), 0)),
            pl.BlockSpec((1, d), lambda bi, i: (0, 0)),
            pl.BlockSpec((ng, POOL_GROUP, POOL_GROUP), lambda bi, i: (0, 0, 0)),
            pl.BlockSpec((1, d), lambda bi, i: (0, 0)),
        ],
        out_specs=pl.BlockSpec((1, ts, d), lambda bi, i: (bi, i, 0)),
        out_shape=jax.ShapeDtypeStruct((b, lp, d), F32),
        scratch_shapes=[pltpu.VMEM((POOL_HALO + ts, d), F32)],
        compiler_params=_params("arbitrary", "arbitrary"),
        name="pool_mixer",
    )(h, h, gain[None, :].astype(F32), pool_w.astype(MXU_DTYPE), pool_scale[None, :].astype(F32))


_PEER_RANKS = PEER_TOPK + 1
_PEER_PAIRS = tuple((a, b) for a in range(_PEER_RANKS) for b in range(_PEER_RANKS)
                    if (a + 1) * (b + 1) <= _PEER_RANKS)
_PEER_CAND_ROWS = -(-len(_PEER_PAIRS) // 8) * 8
_PEER_RANK_ROWS = -(-_PEER_RANKS // 8) * 8


def _extract_maxima(cur_ref, dst_ref, n):
    def body(r, carry):
        cur = cur_ref[...]
        m = jnp.max(cur, axis=0, keepdims=True)
        dst_ref[pl.ds(r, 1), :] = m
        cur_ref[...] = jnp.where(cur == m, -jnp.inf, cur)
        return carry
    lax.fori_loop(0, n, body, 0)


def _peer_kernel(h_ref, g_ref, wq_ref, sk_ref, u_ref, vt_ref, out_ref,
                 hn_ref, e2_ref, e1_ref, phi_ref, acc_ref, p_ref, pre_ref,
                 cur_ref, t1_ref, t2_ref, cand_ref, top_ref):
    e = pl.program_id(1)
    nk = PEER_N_KEYS

    @pl.when(e == 0)
    def _prepare():
        hn_ref[...] = _rms(h_ref[...], g_ref[...]).T.astype(MXU_DTYPE)
        acc_ref[...] = jnp.zeros_like(acc_ref)

        def head(hd, carry):
            wq = wq_ref[pl.ds(pl.multiple_of(hd * 2 * nk, 2 * nk), 2 * nk), :]
            qt = jnp.dot(wq, hn_ref[...], preferred_element_type=F32)
            s1 = jnp.dot(sk_ref[0], qt[:nk].astype(MXU_DTYPE), preferred_element_type=F32)
            s2 = jnp.dot(sk_ref[1], qt[nk:].astype(MXU_DTYPE), preferred_element_type=F32)
            cur_ref[...] = s1
            _extract_maxima(cur_ref, t1_ref, _PEER_RANKS)
            cur_ref[...] = s2
            _extract_maxima(cur_ref, t2_ref, _PEER_RANKS)
            cand_ref[...] = jnp.full(cand_ref.shape, -jnp.inf, F32)
            for idx, (a, b) in enumerate(_PEER_PAIRS):
                cand_ref[idx:idx + 1, :] = t1_ref[a:a + 1, :] + t2_ref[b:b + 1, :]
            _extract_maxima(cand_ref, top_ref, _PEER_RANKS)
            top = top_ref[...]
            best = top[0:1, :]
            z = jnp.sum(jnp.exp(top[0:PEER_TOPK, :] - best), axis=0, keepdims=True)
            tau = 0.5 * (top[PEER_TOPK - 1:PEER_TOPK, :] + top[PEER_TOPK:PEER_TOPK + 1, :])
            phi_ref[pl.ds(hd, 1), :] = jnp.exp(tau - best) / z
            e1 = jnp.exp(s1 - t1_ref[0:1, :])
            for blk in range(nk // SUBLANES):
                e1_ref[hd, blk] = e1[blk * SUBLANES:(blk + 1) * SUBLANES, :]
            e2_ref[hd] = (jnp.exp(s2 - t2_ref[0:1, :]) / z).astype(e2_ref.dtype)
            return carry

        lax.fori_loop(0, PEER_HEADS, head, 0)

    n_tok = hn_ref.shape[1]
    slab = 2 * nk
    n_slabs = u_ref.shape[0] // slab
    rows_per = 2 * SUBLANES

    def score(s):
        pre_ref[s] = jnp.dot(u_ref[s * slab:(s + 1) * slab, :], hn_ref[...],
                             preferred_element_type=F32)

    score(0)
    for s in range(n_slabs):
        if s + 1 < n_slabs:
            score(s + 1)
        for cs in range(n_tok // LANES):
            lanes = slice(cs * LANES, (cs + 1) * LANES)
            phi_rows = [jnp.broadcast_to(phi_ref[hd:hd + 1, lanes], (rows_per, LANES))
                        .astype(e2_ref.dtype) for hd in range(PEER_HEADS)]
            for half in range(slab // nk):
                ii = s * (slab // nk) + half
                e1_rows = [jnp.broadcast_to(e1_ref[hd, e, ii:ii + 1, lanes], (rows_per, LANES))
                           .astype(e2_ref.dtype) for hd in range(PEER_HEADS)]
                for jp in range(nk // rows_per):
                    w = None
                    for hd in range(PEER_HEADS):
                        g = e2_ref[hd, jp * rows_per:(jp + 1) * rows_per, lanes] * e1_rows[hd]
                        keep = g >= phi_rows[hd]
                        w = jnp.where(keep, g, jnp.zeros_like(g)) if w is None else jnp.where(keep, w + g, w)
                    rows = slice(half * nk + jp * rows_per, half * nk + (jp + 1) * rows_per)
                    pre = pre_ref[s, rows, lanes]
                    act = 0.5 * pre * (1.0 + lax.erf(pre * math.sqrt(0.5)))
                    p_ref[s, rows, lanes] = w * act.astype(w.dtype)
        acc_ref[...] += jnp.dot(vt_ref[0, :, s * slab:(s + 1) * slab], p_ref[s],
                                preferred_element_type=F32)

    @pl.when(e == pl.num_programs(1) - 1)
    def _finish():
        out_ref[...] = h_ref[...] + acc_ref[...].T


def _peer_call(h2d, gain, w_q, subkeys, u_tab, v_tab):
    t, d = h2d.shape
    n_exp = u_tab.shape[0]
    c, te = PEER_TOKENS, PEER_EXPERTS
    nk = PEER_N_KEYS
    n_blocks = n_exp // te
    slab = 2 * nk
    once = pl.Buffered(1)
    vt_blocks = v_tab.astype(MXU_DTYPE).reshape(n_blocks, te, d).transpose(0, 2, 1)
    return pl.pallas_call(
        _peer_kernel,
        grid=(t // c, n_blocks),
        in_specs=[
            pl.BlockSpec((c, d), lambda ti, e: (ti, 0), pipeline_mode=once),
            pl.BlockSpec((1, d), lambda ti, e: (0, 0)),
            pl.BlockSpec((PEER_HEADS * 2 * nk, d), lambda ti, e: (0, 0), pipeline_mode=once),
            pl.BlockSpec((2, nk, nk), lambda ti, e: (0, 0, 0)),
            pl.BlockSpec((te, d), lambda ti, e: (e, 0)),
            pl.BlockSpec((1, d, te), lambda ti, e: (e, 0, 0)),
        ],
        out_specs=pl.BlockSpec((c, d), lambda ti, e: (ti, 0)),
        out_shape=jax.ShapeDtypeStruct((t, d), F32),
        scratch_shapes=[
            pltpu.VMEM((d, c), MXU_DTYPE),
            pltpu.VMEM((PEER_HEADS, nk, c), MXU_DTYPE),
            pltpu.VMEM((PEER_HEADS, nk // SUBLANES, SUBLANES, c), F32),
            pltpu.VMEM((PEER_HEADS, c), F32),
            pltpu.VMEM((d, c), F32),
            pltpu.VMEM((te // slab, slab, c), MXU_DTYPE),
            pltpu.VMEM((te // slab, slab, c), F32),
            pltpu.VMEM((nk, c), F32),
            pltpu.VMEM((_PEER_RANK_ROWS, c), F32),
            pltpu.VMEM((_PEER_RANK_ROWS, c), F32),
            pltpu.VMEM((_PEER_CAND_ROWS, c), F32),
            pltpu.VMEM((_PEER_RANK_ROWS, c), F32),
        ],
        compiler_params=_params("arbitrary", "arbitrary"),
        name="peer_ffn",
    )(h2d, gain[None, :].astype(F32), w_q.T.astype(MXU_DTYPE), subkeys.astype(MXU_DTYPE),
      u_tab.astype(MXU_DTYPE), vt_blocks)


def kernel(x, meta, norm_mix, norm_ffn, sb_w_qkv, sb_q_gain, sb_k_gain, sb_w_o, pool_w, pool_scale,
           peer_w_q, peer_subkeys, peer_u, peer_v):
    b, seq, d = x.shape
    depth = norm_mix.shape[0]
    l = N_META + seq
    lp = -(-l // ATTN_TILE) * ATTN_TILE
    while (b * lp) % math.lcm(ROW_TILE, PEER_TOKENS):
        lp += ATTN_TILE
    h = jnp.concatenate([jnp.broadcast_to(meta[None].astype(x.dtype), (b, N_META, d)), x,
                         jnp.zeros((b, lp - l, d), x.dtype)], axis=1)
    for i in range(depth):
        j = i // 2
        if i % 2 == 0:
            h2d = h.reshape(b * lp, d)
            q, k, v = _qkv_call(h2d, norm_mix[i], sb_w_qkv[j], sb_q_gain[j], sb_k_gain[j])
            o = _attn_call(q.reshape(b, lp, d), k.reshape(b, lp, d), v.reshape(b, lp, d))
            h2d = _proj_call(h2d, o.reshape(b * lp, d), sb_w_o[j])
        else:
            h2d = _pool_call(h, norm_mix[i], pool_w[j], pool_scale[j]).reshape(b * lp, d)
        h2d = _peer_call(h2d, norm_ffn[i], peer_w_q[i], peer_subkeys[i], peer_u[i], peer_v[i])
        h = h2d.reshape(b, lp, d)
    return h[:, N_META:l]
```

```python
import functools
import math

import jax
import jax.numpy as jnp
from jax import lax
from jax.experimental import pallas as pl
from jax.experimental.pallas import tpu as pltpu

F32 = jnp.float32
MXU_DTYPE = jnp.bfloat16

RMS_EPS = 1e-6
N_META = 16
SB_HEAD_DIM = 64
POOL_WINDOWS = (2, 4, 8, 16)
POOL_GROUP = 256
PEER_HEADS = 8
PEER_N_KEYS = 128
PEER_TOPK = 16

LANES = 128
SUBLANES = 8
ATTN_TILE = 256
ROW_TILE = 512
PEER_TOKENS = 1024
PEER_EXPERTS = SUBLANES * PEER_N_KEYS
POOL_HALO = 16
VMEM_LIMIT = 56 * 1024 * 1024


def _rms(x, gain):
    ms = jnp.mean(x * x, axis=-1, keepdims=True)
    return x * lax.rsqrt(ms + RMS_EPS) * gain


def _params(*sem, flags=None):
    return pltpu.CompilerParams(dimension_semantics=sem, vmem_limit_bytes=VMEM_LIMIT, flags=flags)


def _qkv_kernel(h_ref, g_ref, w_ref, bd_ref, qg_ref, kg_ref, q_ref, k_ref, v_ref):
    d = h_ref.shape[1]
    hn = _rms(h_ref[...], g_ref[...]).astype(MXU_DTYPE)
    bd = bd_ref[...]
    for src, gref, dst in ((0, qg_ref, q_ref), (1, kg_ref, k_ref)):
        t = jnp.dot(hn, w_ref[:, src * d:(src + 1) * d], preferred_element_type=F32)
        for c in range(d // LANES):
            sl = slice(c * LANES, (c + 1) * LANES)
            tc = t[:, sl]
            ms = jnp.dot((tc * tc).astype(MXU_DTYPE), bd, preferred_element_type=F32)
            dst[:, sl] = (tc * lax.rsqrt(ms + RMS_EPS) * gref[:, sl]).astype(dst.dtype)
    v_ref[...] = jnp.dot(hn, w_ref[:, 2 * d:], preferred_element_type=F32).astype(v_ref.dtype)


def _qkv_call(h2d, gain, w_qkv, q_gain, k_gain):
    t, d = h2d.shape
    scale = LOG2E / math.sqrt(SB_HEAD_DIM)
    reps = d // SB_HEAD_DIM
    qg = jnp.tile(q_gain.astype(F32) * scale, reps)[None, :]
    kg = jnp.tile(k_gain.astype(F32), reps)[None, :]
    idx = jnp.arange(LANES) // SB_HEAD_DIM
    bd = (idx[:, None] == idx[None, :]).astype(F32) / SB_HEAD_DIM
    row = pl.BlockSpec((ROW_TILE, d), lambda i: (i, 0))
    full = lambda shape: pl.BlockSpec(shape, lambda i: (0,) * len(shape))
    out = jax.ShapeDtypeStruct((t, d), MXU_DTYPE)
    return pl.pallas_call(
        _qkv_kernel,
        grid=(t // ROW_TILE,),
        in_specs=[row, full((1, d)), full((d, 3 * d)), full((LANES, LANES)), full((1, d)), full((1, d))],
        out_specs=[row, row, row],
        out_shape=[out, out, out],
        compiler_params=_params("arbitrary"),
        name="qkv_proj",
    )(h2d, gain[None, :].astype(F32), w_qkv.astype(MXU_DTYPE), bd.astype(MXU_DTYPE), qg, kg)


LOG2E = math.log2(math.e)
ATTN_EXIT = 160.0


def _softplus2(z):
    return jnp.maximum(z, 0.0) + jnp.log(1.0 + jnp.exp2(-jnp.abs(z))) * LOG2E


def _suffix_sums(sp, uo):
    if MXU_DTYPE == F32:
        return jnp.dot(sp, uo, preferred_element_type=F32)
    hi = sp.astype(MXU_DTYPE)
    lo = (sp - hi.astype(F32)).astype(MXU_DTYPE)
    return (jnp.dot(hi, uo, preferred_element_type=F32)
            + jnp.dot(lo, uo, preferred_element_type=F32))


def _attn_kernel(q_ref, k_ref, v_ref, uo_ref, o_ref, acc_ref, later_ref, *, tile):
    qi = pl.program_id(2)
    q2 = q_ref[0]
    lane = lax.broadcasted_iota(jnp.int32, (1, LANES), 1)
    row = lax.broadcasted_iota(jnp.int32, (tile, tile), 0)
    col = lax.broadcasted_iota(jnp.int32, (tile, tile), 1)
    causal = col < row
    uo = uo_ref[...]
    qms = [jnp.where((lane >= hh * SB_HEAD_DIM) & (lane < (hh + 1) * SB_HEAD_DIM), q2,
                     jnp.zeros_like(q2)) for hh in range(2)]

    def step(kj, diagonal):
        start = pl.multiple_of(kj * tile, tile)
        kt = k_ref[0, pl.ds(start, tile), :]
        vt = v_ref[0, pl.ds(start, tile), :]
        for hh in range(2):
            z = lax.dot_general(qms[hh], kt, (((1,), (1,)), ((), ())), preferred_element_type=F32)
            sp = _softplus2(z)
            if diagonal:
                sp = jnp.where(causal, sp, 0.0)
            r = _suffix_sums(sp, uo)
            logit = z - sp - r[:, :tile]
            total = r[:, tile:]
            if diagonal:
                a = jnp.where(causal, jnp.exp2(logit), 0.0)
            else:
                later = later_ref[hh]
                a = jnp.exp2(logit - jnp.concatenate([later] * (tile // LANES), axis=1))
            pv = jnp.dot(a.astype(vt.dtype), vt, preferred_element_type=F32)
            if diagonal:
                acc_ref[hh] = pv
                later_ref[hh] = total
            else:
                acc_ref[hh] += pv
                later_ref[hh] = later + total

    step(qi, True)

    def more(state):
        j, live = state
        return (j < qi) & (live > 0)

    def body(state):
        j, _ = state
        step(qi - 1 - j, False)
        reach = jnp.min(jnp.minimum(later_ref[0], later_ref[1]))
        return j + 1, (reach < ATTN_EXIT).astype(jnp.int32)

    lax.while_loop(more, body, (jnp.int32(0), jnp.int32(1)))
    o_ref[0] = jnp.where(lane < SB_HEAD_DIM, acc_ref[0], acc_ref[1]).astype(o_ref.dtype)


def _attn_call(q, k, v):
    b, lp, d = q.shape
    tile = ATTN_TILE
    j = jnp.arange(tile)
    strict = (j[:, None] > j[None, :]).astype(F32)
    uo = jnp.concatenate([strict, jnp.ones((tile, LANES), F32)], axis=1).astype(MXU_DTYPE)
    qspec = pl.BlockSpec((1, tile, LANES), lambda bi, hp, qi: (bi, qi, hp))
    kvspec = pl.BlockSpec((1, lp, LANES), lambda bi, hp, qi: (bi, 0, hp))
    return pl.pallas_call(
        functools.partial(_attn_kernel, tile=tile),
        grid=(b, d // LANES, lp // tile),
        in_specs=[qspec, kvspec, kvspec, pl.BlockSpec(uo.shape, lambda bi, hp, qi: (0, 0))],
        out_specs=qspec,
        out_shape=jax.ShapeDtypeStruct((b, lp, d), MXU_DTYPE),
        scratch_shapes=[pltpu.VMEM((2, tile, LANES), F32),
                        pltpu.VMEM((2, tile, LANES), F32)],
        compiler_params=_params("arbitrary", "arbitrary", "arbitrary"),
        name="sb_attention",
    )(q, k, v, uo)


def _proj_kernel(h_ref, o_ref, w_ref, out_ref):
    out_ref[...] = h_ref[...] + jnp.dot(o_ref[...], w_ref[...], preferred_element_type=F32)


def _proj_call(h2d, o2d, w_o):
    t, d = h2d.shape
    row = pl.BlockSpec((ROW_TILE, d), lambda i: (i, 0))
    return pl.pallas_call(
        _proj_kernel,
        grid=(t // ROW_TILE,),
        in_specs=[row, row, pl.BlockSpec((d, d), lambda i: (0, 0))],
        out_specs=row,
        out_shape=jax.ShapeDtypeStruct((t, d), F32),
        compiler_params=_params("arbitrary"),
        name="attn_out_proj",
    )(h2d, o2d, w_o.astype(MXU_DTYPE))


def _pool_kernel(h_ref, halo_ref, g_ref, w_ref, s_ref, out_ref, ext_ref, *, ts):
    i = pl.program_id(1)
    x = h_ref[0]
    gain = g_ref[...]
    hn = _rms(x, gain)
    halo = jnp.where(i > 0, _rms(halo_ref[0], gain), 0.0)
    ext_ref[0:POOL_HALO, :] = halo
    ext_ref[POOL_HALO:, :] = hn
    pos = i * ts + lax.broadcasted_iota(jnp.int32, (ts, 1), 0)
    for g, w in enumerate(POOL_WINDOWS):
        sl = slice(g * POOL_GROUP, (g + 1) * POOL_GROUP)
        acc = ext_ref[POOL_HALO:, sl]
        for kk in range(1, w):
            acc = acc + ext_ref[POOL_HALO - kk:POOL_HALO - kk + ts, sl]
        cnt = jnp.minimum(pos + 1, w).astype(F32)
        pooled = acc / cnt - ext_ref[POOL_HALO:, sl]
        y = jnp.dot(pooled.astype(MXU_DTYPE), w_ref[g], preferred_element_type=F32)
        out_ref[0, :, sl] = x[:, sl] + y * s_ref[:, sl]


def _pool_call(h, gain, pool_w, pool_scale):
    b, lp, d = h.shape
    ts = ATTN_TILE
    per = ts // POOL_HALO
    ng = len(POOL_WINDOWS)
    return pl.pallas_call(
        functools.partial(_pool_kernel, ts=ts),
        grid=(b, lp // ts),
        in_specs=[
            pl.BlockSpec((1, ts, d), lambda bi, i: (bi, i, 0)),
            pl.BlockSpec((1, POOL_HALO, d), lambda bi, i: (bi, jnp.maximum(i * per - 1, 0), 0)),
            pl.BlockSpec((1, d), lambda bi, i: (0, 0)),
            pl.BlockSpec((ng, POOL_GROUP, POOL_GROUP), lambda bi, i: (0, 0, 0)),
            pl.BlockSpec((1, d), lambda bi, i: (0, 0)),
        ],
        out_specs=pl.BlockSpec((1, ts, d), lambda bi, i: (bi, i, 0)),
        out_shape=jax.ShapeDtypeStruct((b, lp, d), F32),
        scratch_shapes=[pltpu.VMEM((POOL_HALO + ts, d), F32)],
        compiler_params=_params("arbitrary", "arbitrary"),
        name="pool_mixer",
    )(h, h, gain[None, :].astype(F32), pool_w.astype(MXU_DTYPE), pool_scale[None, :].astype(F32))


_PEER_RANKS = PEER_TOPK + 1
_PEER_PAIRS = tuple((a, b) for a in range(_PEER_RANKS) for b in range(_PEER_RANKS)
                    if (a + 1) * (b + 1) <= _PEER_RANKS)
_PEER_CAND_ROWS = -(-len(_PEER_PAIRS) // 8) * 8
_PEER_RANK_ROWS = -(-_PEER_RANKS // 8) * 8


def _sorting_network(n):
    def merge(lo, hi, r):
        step = r * 2
        if step < hi - lo:
            yield from merge(lo, hi, step)
            yield from merge(lo + r, hi, step)
            yield from ((i, i + r) for i in range(lo + r, hi - r, step))
        else:
            yield (lo, lo + r)

    def sort(lo, hi):
        if hi > lo:
            mid = lo + (hi - lo) // 2
            yield from sort(lo, mid)
            yield from sort(mid + 1, hi)
            yield from merge(lo, hi, 1)

    return tuple(sort(0, n - 1))


def _largest_per_column(src_ref, lanes, dst_ref, n):
    groups = src_ref.shape[0] // SUBLANES
    size = 1 << (groups - 1).bit_length()
    lowest = jnp.full((SUBLANES, LANES), -jnp.inf, F32)
    s = [src_ref[g * SUBLANES:(g + 1) * SUBLANES, lanes] for g in range(groups)]
    s += [lowest] * (size - groups)
    for i, j in _sorting_network(size):
        s[i], s[j] = jnp.maximum(s[i], s[j]), jnp.minimum(s[i], s[j])
    s.append(lowest)
    for r in range(n):
        m = jnp.max(s[0], axis=0, keepdims=True)
        dst_ref[r:r + 1, lanes] = m
        if r + 1 < n:
            hit = s[0] == m
            for k in range(min(n - 1 - r, size)):
                s[k] = jnp.where(hit, s[k + 1], s[k])


def _peer_kernel(h_ref, g_ref, wq_ref, sk_ref, u_ref, vt_ref, out_ref,
                 hn_ref, e2_ref, e1_ref, phi_ref, acc_ref, p_ref, pre_ref,
                 s1_ref, s2_ref, t1_ref, t2_ref, cand_ref, top_ref):
    e = pl.program_id(1)
    nk = PEER_N_KEYS

    @pl.when(e == 0)
    def _prepare():
        hn_ref[...] = _rms(h_ref[...], g_ref[...]).T.astype(MXU_DTYPE)
        acc_ref[...] = jnp.zeros_like(acc_ref)

        def head(hd, carry):
            wq = wq_ref[pl.ds(pl.multiple_of(hd * 2 * nk, 2 * nk), 2 * nk), :]
            qt = jnp.dot(wq, hn_ref[...], preferred_element_type=F32)
            s1_ref[...] = jnp.dot(sk_ref[0], qt[:nk].astype(MXU_DTYPE), preferred_element_type=F32)
            s2_ref[...] = jnp.dot(sk_ref[1], qt[nk:].astype(MXU_DTYPE), preferred_element_type=F32)
            strips = [slice(cs * LANES, (cs + 1) * LANES) for cs in range(s1_ref.shape[1] // LANES)]
            for lanes in strips:
                _largest_per_column(s1_ref, lanes, t1_ref, _PEER_RANKS)
                _largest_per_column(s2_ref, lanes, t2_ref, _PEER_RANKS)
            cand_ref[...] = jnp.full(cand_ref.shape, -jnp.inf, F32)
            for idx, (a, b) in enumerate(_PEER_PAIRS):
                cand_ref[idx:idx + 1, :] = t1_ref[a:a + 1, :] + t2_ref[b:b + 1, :]
            for lanes in strips:
                _largest_per_column(cand_ref, lanes, top_ref, _PEER_RANKS)
            s1 = s1_ref[...]
            s2 = s2_ref[...]
            top = top_ref[...]
            best = top[0:1, :]
            z = jnp.sum(jnp.exp(top[0:PEER_TOPK, :] - best), axis=0, keepdims=True)
            tau = 0.5 * (top[PEER_TOPK - 1:PEER_TOPK, :] + top[PEER_TOPK:PEER_TOPK + 1, :])
            phi_ref[pl.ds(hd, 1), :] = jnp.exp(tau - best) / z
            e1 = jnp.exp(s1 - t1_ref[0:1, :])
            for blk in range(nk // SUBLANES):
                e1_ref[hd, blk] = e1[blk * SUBLANES:(blk + 1) * SUBLANES, :]
            e2_ref[hd] = (jnp.exp(s2 - t2_ref[0:1, :]) / z).astype(e2_ref.dtype)
            return carry

        lax.fori_loop(0, PEER_HEADS, head, 0)

    n_tok = hn_ref.shape[1]
    slab = 2 * nk
    n_slabs = u_ref.shape[0] // slab
    rows_per = 2 * SUBLANES

    def score(s):
        pre_ref[s] = jnp.dot(u_ref[s * slab:(s + 1) * slab, :], hn_ref[...],
                             preferred_element_type=F32)

    score(0)
    for s in range(n_slabs):
        if s + 1 < n_slabs:
            score(s + 1)
        for cs in range(n_tok // LANES):
            lanes = slice(cs * LANES, (cs + 1) * LANES)
            phi_rows = [jnp.broadcast_to(phi_ref[hd:hd + 1, lanes], (rows_per, LANES))
                        .astype(e2_ref.dtype) for hd in range(PEER_HEADS)]
            for half in range(slab // nk):
                ii = s * (slab // nk) + half
                e1_rows = [jnp.broadcast_to(e1_ref[hd, e, ii:ii + 1, lanes], (rows_per, LANES))
                           .astype(e2_ref.dtype) for hd in range(PEER_HEADS)]
                for jp in range(nk // rows_per):
                    w = None
                    for hd in range(PEER_HEADS):
                        g = e2_ref[hd, jp * rows_per:(jp + 1) * rows_per, lanes] * e1_rows[hd]
                        term = jnp.where(g >= phi_rows[hd], g, jnp.zeros_like(g))
                        w = term if w is None else w + term
                    r0 = half * nk + jp * rows_per
                    pre = pre_ref[s, r0:r0 + rows_per, lanes]
                    act = 0.5 * pre * (1.0 + lax.erf(pre * math.sqrt(0.5)))
                    p_ref[s * slab + r0:s * slab + r0 + rows_per, lanes] = w * act.astype(w.dtype)
        if s % 2 == 1:
            done = slice((s - 1) * slab, (s + 1) * slab)
            acc_ref[...] += jnp.dot(vt_ref[0, :, done], p_ref[done, :], preferred_element_type=F32)

    @pl.when(e == pl.num_programs(1) - 1)
    def _finish():
        out_ref[...] = h_ref[...] + acc_ref[...].T


def _peer_call(h2d, gain, w_q, subkeys, u_tab, v_tab):
    t, d = h2d.shape
    n_exp = u_tab.shape[0]
    c, te = PEER_TOKENS, PEER_EXPERTS
    nk = PEER_N_KEYS
    n_blocks = n_exp // te
    slab = 2 * nk
    once = pl.Buffered(1)
    vt_blocks = v_tab.astype(MXU_DTYPE).reshape(n_blocks, te, d).transpose(0, 2, 1)
    return pl.pallas_call(
        _peer_kernel,
        grid=(t // c, n_blocks),
        in_specs=[
            pl.BlockSpec((c, d), lambda ti, e: (ti, 0), pipeline_mode=once),
            pl.BlockSpec((1, d), lambda ti, e: (0, 0)),
            pl.BlockSpec((PEER_HEADS * 2 * nk, d), lambda ti, e: (0, 0), pipeline_mode=once),
            pl.BlockSpec((2, nk, nk), lambda ti, e: (0, 0, 0)),
            pl.BlockSpec((te, d), lambda ti, e: (e, 0)),
            pl.BlockSpec((1, d, te), lambda ti, e: (e, 0, 0)),
        ],
        out_specs=pl.BlockSpec((c, d), lambda ti, e: (ti, 0)),
        out_shape=jax.ShapeDtypeStruct((t, d), F32),
        scratch_shapes=[
            pltpu.VMEM((d, c), MXU_DTYPE),
            pltpu.VMEM((PEER_HEADS, nk, c), MXU_DTYPE),
            pltpu.VMEM((PEER_HEADS, nk // SUBLANES, SUBLANES, c), F32),
            pltpu.VMEM((PEER_HEADS, c), F32),
            pltpu.VMEM((d, c), F32),
            pltpu.VMEM((te, c), MXU_DTYPE),
            pltpu.VMEM((te // slab, slab, c), F32),
            pltpu.VMEM((nk, c), F32),
            pltpu.VMEM((nk, c), F32),
            pltpu.VMEM((_PEER_RANK_ROWS, c), F32),
            pltpu.VMEM((_PEER_RANK_ROWS, c), F32),
            pltpu.VMEM((_PEER_CAND_ROWS, c), F32),
            pltpu.VMEM((_PEER_RANK_ROWS, c), F32),
        ],
        compiler_params=_params("arbitrary", "arbitrary"),
        name="peer_ffn",
    )(h2d, gain[None, :].astype(F32), w_q.T.astype(MXU_DTYPE), subkeys.astype(MXU_DTYPE),
      u_tab.astype(MXU_DTYPE), vt_blocks)


def kernel(x, meta, norm_mix, norm_ffn, sb_w_qkv, sb_q_gain, sb_k_gain, sb_w_o, pool_w, pool_scale,
           peer_w_q, peer_subkeys, peer_u, peer_v):
    b, seq, d = x.shape
    depth = norm_mix.shape[0]
    l = N_META + seq
    lp = -(-l // ATTN_TILE) * ATTN_TILE
    while (b * lp) % math.lcm(ROW_TILE, PEER_TOKENS):
        lp += ATTN_TILE
    h = jnp.concatenate([jnp.broadcast_to(meta[None].astype(x.dtype), (b, N_META, d)), x,
                         jnp.zeros((b, lp - l, d), x.dtype)], axis=1)
    for i in range(depth):
        j = i // 2
        if i % 2 == 0:
            h2d = h.reshape(b * lp, d)
            q, k, v = _qkv_call(h2d, norm_mix[i], sb_w_qkv[j], sb_q_gain[j], sb_k_gain[j])
            o = _attn_call(q.reshape(b, lp, d), k.reshape(b, lp, d), v.reshape(b, lp, d))
            h2d = _proj_call(h2d, o.reshape(b * lp, d), sb_w_o[j])
        else:
            h2d = _pool_call(h, norm_mix[i], pool_w[j], pool_scale[j]).reshape(b * lp, d)
        h2d = _peer_call(h2d, norm_ffn[i], peer_w_q[i], peer_subkeys[i], peer_u[i], peer_v[i])
        h = h2d.reshape(b, lp, d)
    return h[:, N_META:l]
```

```python
import functools
import math

import jax
import jax.numpy as jnp
from jax import lax
from jax.experimental import pallas as pl
from jax.experimental.pallas import tpu as pltpu

F32 = jnp.float32
MXU_DTYPE = jnp.bfloat16

RMS_EPS = 1e-6
N_META = 16
SB_HEAD_DIM = 64
POOL_WINDOWS = (2, 4, 8, 16)
POOL_GROUP = 256
PEER_HEADS = 8
PEER_N_KEYS = 128
PEER_TOPK = 16

LANES = 128
SUBLANES = 8
ATTN_TILE = 256
ROW_TILE = 1024
PEER_TOKENS = 1024
PEER_EXPERTS = SUBLANES * PEER_N_KEYS
PEER_SLAB = 8 * PEER_N_KEYS
POOL_HALO = 16
VMEM_LIMIT = 56 * 1024 * 1024


def _rms(x, gain):
    ms = jnp.mean(x * x, axis=-1, keepdims=True)
    return x * lax.rsqrt(ms + RMS_EPS) * gain


def _params(*sem):
    return pltpu.CompilerParams(dimension_semantics=sem, vmem_limit_bytes=VMEM_LIMIT)


def _qkv_kernel(h_ref, g_ref, w_ref, bd_ref, qg_ref, kg_ref, q_ref, k_ref, v_ref):
    d = h_ref.shape[1]
    hn = _rms(h_ref[...], g_ref[...]).astype(MXU_DTYPE)
    bd = bd_ref[...]
    for src, gref, dst in ((0, qg_ref, q_ref), (1, kg_ref, k_ref)):
        t = jnp.dot(hn, w_ref[:, src * d:(src + 1) * d], preferred_element_type=F32)
        for c in range(d // LANES):
            sl = slice(c * LANES, (c + 1) * LANES)
            tc = t[:, sl]
            ms = jnp.dot((tc * tc).astype(MXU_DTYPE), bd, preferred_element_type=F32)
            dst[:, sl] = (tc * lax.rsqrt(ms + RMS_EPS) * gref[:, sl]).astype(dst.dtype)
    v_ref[...] = jnp.dot(hn, w_ref[:, 2 * d:], preferred_element_type=F32).astype(v_ref.dtype)


def _qkv_call(h2d, gain, w_qkv, q_gain, k_gain):
    t, d = h2d.shape
    scale = LOG2E / math.sqrt(SB_HEAD_DIM)
    reps = d // SB_HEAD_DIM
    qg = jnp.tile(q_gain.astype(F32) * scale, reps)[None, :]
    kg = jnp.tile(k_gain.astype(F32), reps)[None, :]
    idx = jnp.arange(LANES) // SB_HEAD_DIM
    bd = (idx[:, None] == idx[None, :]).astype(F32) / SB_HEAD_DIM
    row = pl.BlockSpec((ROW_TILE, d), lambda i: (i, 0))
    full = lambda shape: pl.BlockSpec(shape, lambda i: (0,) * len(shape))
    out = jax.ShapeDtypeStruct((t, d), MXU_DTYPE)
    return pl.pallas_call(
        _qkv_kernel,
        grid=(t // ROW_TILE,),
        in_specs=[row, full((1, d)), full((d, 3 * d)), full((LANES, LANES)), full((1, d)), full((1, d))],
        out_specs=[row, row, row],
        out_shape=[out, out, out],
        compiler_params=_params("arbitrary"),
        name="qkv_proj",
    )(h2d, gain[None, :].astype(F32), w_qkv.astype(MXU_DTYPE), bd.astype(MXU_DTYPE), qg, kg)


LOG2E = math.log2(math.e)
ATTN_EXIT = 160.0


def _softplus2(z):
    return jnp.maximum(z, 0.0) + jnp.log(1.0 + jnp.exp2(-jnp.abs(z))) * LOG2E


def _suffix_sums(sp, later_keys):
    rows = sp.shape[0]
    hi = sp.astype(MXU_DTYPE)
    lo = (sp - hi.astype(F32)).astype(MXU_DTYPE)
    r = jnp.dot(jnp.concatenate([hi, lo], axis=0), later_keys, preferred_element_type=F32)
    return r[:rows] + r[rows:]


def _attn_kernel(q_ref, k_ref, v_ref, lk_ref, o_ref, acc_ref, later_ref, *, tile):
    qi = pl.program_id(2)
    q2 = q_ref[0]
    lane = lax.broadcasted_iota(jnp.int32, (1, LANES), 1)
    row = lax.broadcasted_iota(jnp.int32, (2 * tile, tile), 0)
    col = lax.broadcasted_iota(jnp.int32, (2 * tile, tile), 1)
    causal = col < (row & (tile - 1))
    later_keys = lk_ref[...]
    qs = jnp.concatenate([jnp.where(lane < SB_HEAD_DIM, q2, jnp.zeros_like(q2)),
                          jnp.where(lane >= SB_HEAD_DIM, q2, jnp.zeros_like(q2))], axis=0)

    def step(kj, diagonal):
        start = pl.multiple_of(kj * tile, tile)
        kt = k_ref[0, pl.ds(start, tile), :]
        vt = v_ref[0, pl.ds(start, tile), :]
        z = lax.dot_general(qs, kt, (((1,), (1,)), ((), ())), preferred_element_type=F32)
        sp = _softplus2(z)
        if diagonal:
            sp = jnp.where(causal, sp, 0.0)
        r = _suffix_sums(sp, later_keys)
        logit = z - sp - r[:, :tile]
        total = r[:, tile:]
        if diagonal:
            a = jnp.where(causal, jnp.exp2(logit), 0.0)
        else:
            later = later_ref[...]
            a = jnp.exp2(logit - jnp.concatenate([later] * (tile // LANES), axis=1))
        pv = jnp.dot(a.astype(vt.dtype), vt, preferred_element_type=F32)
        if diagonal:
            acc_ref[...] = pv
            later_ref[...] = total
        else:
            acc_ref[...] += pv
            later_ref[...] = later + total

    step(qi, True)

    def more(state):
        j, live = state
        return (j < qi) & (live > 0)

    def body(state):
        j, _ = state
        step(qi - 1 - j, False)
        reach = jnp.min(later_ref[...])
        return j + 1, (reach < ATTN_EXIT).astype(jnp.int32)

    lax.while_loop(more, body, (jnp.int32(0), jnp.int32(1)))
    o_ref[0] = jnp.where(lane < SB_HEAD_DIM, acc_ref[:tile], acc_ref[tile:]).astype(o_ref.dtype)


def _attn_call(q, k, v):
    b, lp, d = q.shape
    tile = ATTN_TILE
    j = jnp.arange(tile)
    strict = (j[:, None] > j[None, :]).astype(F32)
    later_keys = jnp.concatenate([strict, jnp.ones((tile, LANES), F32)], axis=1).astype(MXU_DTYPE)
    qspec = pl.BlockSpec((1, tile, LANES), lambda bi, hp, qi: (bi, qi, hp))
    kvspec = pl.BlockSpec((1, lp, LANES), lambda bi, hp, qi: (bi, 0, hp))
    return pl.pallas_call(
        functools.partial(_attn_kernel, tile=tile),
        grid=(b, d // LANES, lp // tile),
        in_specs=[qspec, kvspec, kvspec, pl.BlockSpec((tile, tile + LANES), lambda bi, hp, qi: (0, 0))],
        out_specs=qspec,
        out_shape=jax.ShapeDtypeStruct((b, lp, d), MXU_DTYPE),
        scratch_shapes=[pltpu.VMEM((2 * tile, LANES), F32),
                        pltpu.VMEM((2 * tile, LANES), F32)],
        compiler_params=_params("arbitrary", "arbitrary", "arbitrary"),
        name="sb_attention",
    )(q, k, v, later_keys)


def _proj_kernel(h_ref, o_ref, w_ref, out_ref):
    out_ref[...] = h_ref[...] + jnp.dot(o_ref[...], w_ref[...], preferred_element_type=F32)


def _proj_call(h2d, o2d, w_o):
    t, d = h2d.shape
    row = pl.BlockSpec((ROW_TILE, d), lambda i: (i, 0))
    return pl.pallas_call(
        _proj_kernel,
        grid=(t // ROW_TILE,),
        in_specs=[row, row, pl.BlockSpec((d, d), lambda i: (0, 0))],
        out_specs=row,
        out_shape=jax.ShapeDtypeStruct((t, d), F32),
        compiler_params=_params("arbitrary"),
        name="attn_out_proj",
    )(h2d, o2d, w_o.astype(MXU_DTYPE))


def _pool_kernel(h_ref, halo_ref, g_ref, w_ref, s_ref, out_ref, ext_ref, *, ts):
    i = pl.program_id(1)
    x = h_ref[0]
    gain = g_ref[...]
    hn = _rms(x, gain)
    halo = jnp.where(i > 0, _rms(halo_ref[0], gain), 0.0)
    ext_ref[0:POOL_HALO, :] = halo
    ext_ref[POOL_HALO:, :] = hn
    pos = i * ts + lax.broadcasted_iota(jnp.int32, (ts, 1), 0)
    for g, w in enumerate(POOL_WINDOWS):
        sl = slice(g * POOL_GROUP, (g + 1) * POOL_GROUP)
        acc = ext_ref[POOL_HALO:, sl]
        for kk in range(1, w):
            acc = acc + ext_ref[POOL_HALO - kk:POOL_HALO - kk + ts, sl]
        cnt = jnp.minimum(pos + 1, w).astype(F32)
        pooled = acc / cnt - ext_ref[POOL_HALO:, sl]
        y = jnp.dot(pooled.astype(MXU_DTYPE), w_ref[g], preferred_element_type=F32)
        out_ref[0, :, sl] = x[:, sl] + y * s_ref[:, sl]


def _pool_call(h, gain, pool_w, pool_scale):
    b, lp, d = h.shape
    ts = ATTN_TILE
    per = ts // POOL_HALO
    ng = len(POOL_WINDOWS)
    return pl.pallas_call(
        functools.partial(_pool_kernel, ts=ts),
        grid=(b, lp // ts),
        in_specs=[
            pl.BlockSpec((1, ts, d), lambda bi, i: (bi, i, 0)),
            pl.BlockSpec((1, POOL_HALO, d), lambda bi, i: (bi, jnp.maximum(i * per - 1, 0), 0)),
            pl.BlockSpec((1, d), lambda bi, i: (0, 0)),
            pl.BlockSpec((ng, POOL_GROUP, POOL_GROUP), lambda bi, i: (0, 0, 0)),
            pl.BlockSpec((1, d), lambda bi, i: (0, 0)),
        ],
        out_specs=pl.BlockSpec((1, ts, d), lambda bi, i: (bi, i, 0)),
        out_shape=jax.ShapeDtypeStruct((b, lp, d), F32),
        scratch_shapes=[pltpu.VMEM((POOL_HALO + ts, d), F32)],
        compiler_params=_params("arbitrary", "arbitrary"),
        name="pool_mixer",
    )(h, h, gain[None, :].astype(F32), pool_w.astype(MXU_DTYPE), pool_scale[None, :].astype(F32))


_PEER_RANKS = PEER_TOPK + 1
_PEER_PAIRS = tuple((a, b) for a in range(_PEER_RANKS) for b in range(_PEER_RANKS)
                    if (a + 1) * (b + 1) <= _PEER_RANKS)
_PEER_CAND_ROWS = -(-len(_PEER_PAIRS) // 8) * 8
_PEER_RANK_ROWS = -(-_PEER_RANKS // 8) * 8


def _sorting_network(n):
    def merge(lo, hi, r):
        step = r * 2
        if step < hi - lo:
            yield from merge(lo, hi, step)
            yield from merge(lo + r, hi, step)
            yield from ((i, i + r) for i in range(lo + r, hi - r, step))
        else:
            yield (lo, lo + r)

    def sort(lo, hi):
        if hi > lo:
            mid = lo + (hi - lo) // 2
            yield from sort(lo, mid)
            yield from sort(mid + 1, hi)
            yield from merge(lo, hi, 1)

    return tuple(sort(0, n - 1))


def _largest_per_column(src_ref, lanes, dst_ref, n):
    groups = src_ref.shape[0] // SUBLANES
    size = 1 << (groups - 1).bit_length()
    lowest = jnp.full((SUBLANES, LANES), -jnp.inf, F32)
    s = [src_ref[g * SUBLANES:(g + 1) * SUBLANES, lanes] for g in range(groups)]
    s += [lowest] * (size - groups)
    for i, j in _sorting_network(size):
        s[i], s[j] = jnp.maximum(s[i], s[j]), jnp.minimum(s[i], s[j])
    s.append(lowest)
    for r in range(n):
        m = jnp.max(s[0], axis=0, keepdims=True)
        dst_ref[r:r + 1, lanes] = m
        if r + 1 < n:
            hit = s[0] == m
            for k in range(min(n - 1 - r, size)):
                s[k] = jnp.where(hit, s[k + 1], s[k])


def _peer_kernel(h_ref, g_ref, wq_ref, sk_ref, u_ref, vt_ref, out_ref,
                 hn_ref, e2_ref, e1_ref, phi_ref, acc_ref, p_ref, pre_ref,
                 s1_ref, s2_ref, t1_ref, t2_ref, cand_ref, top_ref):
    e = pl.program_id(1)
    nk = PEER_N_KEYS

    @pl.when(e == 0)
    def _prepare():
        hn_ref[...] = _rms(h_ref[...], g_ref[...]).T.astype(MXU_DTYPE)
        acc_ref[...] = jnp.zeros_like(acc_ref)

        def head(hd, q0):
            q1 = pre_ref[0, pl.ds(pl.multiple_of(q0, nk), nk), :].astype(MXU_DTYPE)
            q2 = pre_ref[0, pl.ds(pl.multiple_of(q0 + nk, nk), nk), :].astype(MXU_DTYPE)
            s1_ref[...] = jnp.dot(sk_ref[0], q1, preferred_element_type=F32)
            s2_ref[...] = jnp.dot(sk_ref[1], q2, preferred_element_type=F32)
            strips = [slice(cs * LANES, (cs + 1) * LANES) for cs in range(s1_ref.shape[1] // LANES)]
            for lanes in strips:
                _largest_per_column(s1_ref, lanes, t1_ref, _PEER_RANKS)
                _largest_per_column(s2_ref, lanes, t2_ref, _PEER_RANKS)
            cand_ref[...] = jnp.full(cand_ref.shape, -jnp.inf, F32)
            for idx, (a, b) in enumerate(_PEER_PAIRS):
                cand_ref[idx:idx + 1, :] = t1_ref[a:a + 1, :] + t2_ref[b:b + 1, :]
            for lanes in strips:
                _largest_per_column(cand_ref, lanes, top_ref, _PEER_RANKS)
            s1 = s1_ref[...]
            s2 = s2_ref[...]
            top = top_ref[...]
            best = top[0:1, :]
            z = jnp.sum(jnp.exp(top[0:PEER_TOPK, :] - best), axis=0, keepdims=True)
            tau = 0.5 * (top[PEER_TOPK - 1:PEER_TOPK, :] + top[PEER_TOPK:PEER_TOPK + 1, :])
            phi_ref[pl.ds(hd, 1), :] = jnp.exp(tau - best) / z
            e1 = jnp.exp(s1 - t1_ref[0:1, :])
            for blk in range(nk // SUBLANES):
                e1_ref[hd, blk] = e1[blk * SUBLANES:(blk + 1) * SUBLANES, :]
            e2_ref[hd] = (jnp.exp(s2 - t2_ref[0:1, :]) / z).astype(e2_ref.dtype)

        rows = pre_ref.shape[1]
        per_group = rows // (2 * nk)
        for grp in range(PEER_HEADS // per_group):
            pre_ref[0] = jnp.dot(wq_ref[grp * rows:(grp + 1) * rows, :], hn_ref[...],
                                 preferred_element_type=F32)

            def body(k, carry, grp=grp):
                head(grp * per_group + k, k * 2 * nk)
                return carry

            lax.fori_loop(0, per_group, body, 0)

    n_tok = hn_ref.shape[1]
    slab = PEER_SLAB
    n_slabs = u_ref.shape[0] // slab
    rows_per = 2 * SUBLANES

    def score(s):
        pre_ref[s] = jnp.dot(u_ref[s * slab:(s + 1) * slab, :], hn_ref[...],
                             preferred_element_type=F32)

    score(0)
    for s in range(n_slabs):
        if s + 1 < n_slabs:
            score(s + 1)
        for cs in range(n_tok // LANES):
            lanes = slice(cs * LANES, (cs + 1) * LANES)
            phi_rows = [jnp.broadcast_to(phi_ref[hd:hd + 1, lanes], (rows_per, LANES))
                        .astype(e2_ref.dtype) for hd in range(PEER_HEADS)]
            for half in range(slab // nk):
                ii = s * (slab // nk) + half
                e1_rows = [jnp.broadcast_to(e1_ref[hd, e, ii:ii + 1, lanes], (rows_per, LANES))
                           .astype(e2_ref.dtype) for hd in range(PEER_HEADS)]
                for jp in range(nk // rows_per):
                    w = None
                    for hd in range(PEER_HEADS):
                        g = e2_ref[hd, jp * rows_per:(jp + 1) * rows_per, lanes] * e1_rows[hd]
                        term = jnp.where(g >= phi_rows[hd], g, jnp.zeros_like(g))
                        w = term if w is None else w + term
                    r0 = half * nk + jp * rows_per
                    pre = pre_ref[s, r0:r0 + rows_per, lanes]
                    act = 0.5 * pre * (1.0 + lax.erf(pre * math.sqrt(0.5)))
                    p_ref[s * slab + r0:s * slab + r0 + rows_per, lanes] = w * act.astype(w.dtype)
        done = slice(s * slab, (s + 1) * slab)
        acc_ref[...] += jnp.dot(vt_ref[0, :, done], p_ref[done, :], preferred_element_type=F32)

    @pl.when(e == pl.num_programs(1) - 1)
    def _finish():
        out_ref[...] = h_ref[...] + acc_ref[...].T


def _peer_call(h2d, gain, w_q, subkeys, u_tab, v_tab):
    t, d = h2d.shape
    n_exp = u_tab.shape[0]
    c, te = PEER_TOKENS, PEER_EXPERTS
    nk = PEER_N_KEYS
    n_blocks = n_exp // te
    slab = PEER_SLAB
    once = pl.Buffered(1)
    vt_blocks = v_tab.astype(MXU_DTYPE).reshape(n_blocks, te, d).transpose(0, 2, 1)
    return pl.pallas_call(
        _peer_kernel,
        grid=(t // c, n_blocks),
        in_specs=[
            pl.BlockSpec((c, d), lambda ti, e: (ti, 0), pipeline_mode=once),
            pl.BlockSpec((1, d), lambda ti, e: (0, 0)),
            pl.BlockSpec((PEER_HEADS * 2 * nk, d), lambda ti, e: (0, 0), pipeline_mode=once),
            pl.BlockSpec((2, nk, nk), lambda ti, e: (0, 0, 0)),
            pl.BlockSpec((te, d), lambda ti, e: (e, 0)),
            pl.BlockSpec((1, d, te), lambda ti, e: (e, 0, 0)),
        ],
        out_specs=pl.BlockSpec((c, d), lambda ti, e: (ti, 0)),
        out_shape=jax.ShapeDtypeStruct((t, d), F32),
        scratch_shapes=[
            pltpu.VMEM((d, c), MXU_DTYPE),
            pltpu.VMEM((PEER_HEADS, nk, c), MXU_DTYPE),
            pltpu.VMEM((PEER_HEADS, nk // SUBLANES, SUBLANES, c), F32),
            pltpu.VMEM((PEER_HEADS, c), F32),
            pltpu.VMEM((d, c), F32),
            pltpu.VMEM((te, c), MXU_DTYPE),
            pltpu.VMEM((te // slab, slab, c), F32),
            pltpu.VMEM((nk, c), F32),
            pltpu.VMEM((nk, c), F32),
            pltpu.VMEM((_PEER_RANK_ROWS, c), F32),
            pltpu.VMEM((_PEER_RANK_ROWS, c), F32),
            pltpu.VMEM((_PEER_CAND_ROWS, c), F32),
            pltpu.VMEM((_PEER_RANK_ROWS, c), F32),
        ],
        compiler_params=_params("arbitrary", "arbitrary"),
        name="peer_ffn",
    )(h2d, gain[None, :].astype(F32), w_q.T.astype(MXU_DTYPE), subkeys.astype(MXU_DTYPE),
      u_tab.astype(MXU_DTYPE), vt_blocks)


def kernel(x, meta, norm_mix, norm_ffn, sb_w_qkv, sb_q_gain, sb_k_gain, sb_w_o, pool_w, pool_scale,
           peer_w_q, peer_subkeys, peer_u, peer_v):
    b, seq, d = x.shape
    depth = norm_mix.shape[0]
    l = N_META + seq
    lp = -(-l // ATTN_TILE) * ATTN_TILE
    while (b * lp) % math.lcm(ROW_TILE, PEER_TOKENS):
        lp += ATTN_TILE
    h = jnp.concatenate([jnp.broadcast_to(meta[None].astype(x.dtype), (b, N_META, d)), x,
                         jnp.zeros((b, lp - l, d), x.dtype)], axis=1)
    for i in range(depth):
        j = i // 2
        if i % 2 == 0:
            h2d = h.reshape(b * lp, d)
            q, k, v = _qkv_call(h2d, norm_mix[i], sb_w_qkv[j], sb_q_gain[j], sb_k_gain[j])
            o = _attn_call(q.reshape(b, lp, d), k.reshape(b, lp, d), v.reshape(b, lp, d))
            h2d = _proj_call(h2d, o.reshape(b * lp, d), sb_w_o[j])
        else:
            h2d = _pool_call(h, norm_mix[i], pool_w[j], pool_scale[j]).reshape(b * lp, d)
        h2d = _peer_call(h2d, norm_ffn[i], peer_w_q[i], peer_subkeys[i], peer_u[i], peer_v[i])
        h = h2d.reshape(b, lp, d)
    return h[:, N_META:l]
```

```python
import functools
import math

import jax
import jax.numpy as jnp
from jax import lax
from jax.experimental import pallas as pl
from jax.experimental.pallas import tpu as pltpu

F32 = jnp.float32
MXU_DTYPE = jnp.bfloat16

RMS_EPS = 1e-6
N_META = 16
SB_HEAD_DIM = 64
POOL_WINDOWS = (2, 4, 8, 16)
POOL_GROUP = 256
PEER_HEADS = 8
PEER_N_KEYS = 128
PEER_TOPK = 16

LANES = 128
SUBLANES = 8
ATTN_TILE = 256
ROW_TILE = 1024
PEER_TOKENS = 1024
PEER_EXPERTS = SUBLANES * PEER_N_KEYS
POOL_HALO = 16
VMEM_LIMIT = 56 * 1024 * 1024


def _rms(x, gain):
    ms = jnp.mean(x * x, axis=-1, keepdims=True)
    return x * lax.rsqrt(ms + RMS_EPS) * gain


def _params(*sem):
    return pltpu.CompilerParams(dimension_semantics=sem, vmem_limit_bytes=VMEM_LIMIT)


def _qkv_kernel(h_ref, g_ref, w_ref, bd_ref, qg_ref, kg_ref, q_ref, k_ref, v_ref):
    d = h_ref.shape[1]
    hn = _rms(h_ref[...], g_ref[...]).astype(MXU_DTYPE)
    bd = bd_ref[...]
    for src, gref, dst in ((0, qg_ref, q_ref), (1, kg_ref, k_ref)):
        t = jnp.dot(hn, w_ref[:, src * d:(src + 1) * d], preferred_element_type=F32)
        for c in range(d // LANES):
            sl = slice(c * LANES, (c + 1) * LANES)
            tc = t[:, sl]
            ms = jnp.dot((tc * tc).astype(MXU_DTYPE), bd, preferred_element_type=F32)
            dst[:, sl] = (tc * lax.rsqrt(ms + RMS_EPS) * gref[:, sl]).astype(dst.dtype)
    v_ref[...] = jnp.dot(hn, w_ref[:, 2 * d:], preferred_element_type=F32).astype(v_ref.dtype)


def _qkv_call(h2d, gain, w_qkv, q_gain, k_gain):
    t, d = h2d.shape
    scale = LOG2E / math.sqrt(SB_HEAD_DIM)
    reps = d // SB_HEAD_DIM
    qg = jnp.tile(q_gain.astype(F32) * scale, reps)[None, :]
    kg = jnp.tile(k_gain.astype(F32), reps)[None, :]
    idx = jnp.arange(LANES) // SB_HEAD_DIM
    bd = (idx[:, None] == idx[None, :]).astype(F32) / SB_HEAD_DIM
    row = pl.BlockSpec((ROW_TILE, d), lambda i: (i, 0))
    full = lambda shape: pl.BlockSpec(shape, lambda i: (0,) * len(shape))
    out = jax.ShapeDtypeStruct((t, d), MXU_DTYPE)
    return pl.pallas_call(
        _qkv_kernel,
        grid=(t // ROW_TILE,),
        in_specs=[row, full((1, d)), full((d, 3 * d)), full((LANES, LANES)), full((1, d)), full((1, d))],
        out_specs=[row, row, row],
        out_shape=[out, out, out],
        compiler_params=_params("arbitrary"),
        name="qkv_proj",
    )(h2d, gain[None, :].astype(F32), w_qkv.astype(MXU_DTYPE), bd.astype(MXU_DTYPE), qg, kg)


LOG2E = math.log2(math.e)
ATTN_EXIT = 160.0


def _softplus2(z):
    return jnp.maximum(z, 0.0) + jnp.log(1.0 + jnp.exp2(-jnp.abs(z))) * LOG2E


def _suffix_sums(sp, later_keys):
    rows = sp.shape[0]
    hi = sp.astype(MXU_DTYPE)
    lo = (sp - hi.astype(F32)).astype(MXU_DTYPE)
    r = jnp.dot(jnp.concatenate([hi, lo], axis=0), later_keys, preferred_element_type=F32)
    return r[:rows] + r[rows:]


def _attn_kernel(q_ref, k_ref, v_ref, lk_ref, o_ref, acc_ref, later_ref, *, tile):
    lane = lax.broadcasted_iota(jnp.int32, (1, LANES), 1)
    row = lax.broadcasted_iota(jnp.int32, (2 * tile, tile), 0)
    col = lax.broadcasted_iota(jnp.int32, (2 * tile, tile), 1)
    causal = col < (row & (tile - 1))
    later_keys = lk_ref[...]

    def query_tile(qi, carry):
        _attn_query_tile(qi, q_ref, k_ref, v_ref, o_ref, acc_ref, later_ref,
                         lane, causal, later_keys, tile)
        return carry

    lax.fori_loop(0, q_ref.shape[1] // tile, query_tile, 0)


def _attn_query_tile(qi, q_ref, k_ref, v_ref, o_ref, acc_ref, later_ref, lane, causal, later_keys, tile):
    rows = pl.ds(pl.multiple_of(qi * tile, tile), tile)
    q2 = q_ref[0, rows, :]
    qs = jnp.concatenate([jnp.where(lane < SB_HEAD_DIM, q2, jnp.zeros_like(q2)),
                          jnp.where(lane >= SB_HEAD_DIM, q2, jnp.zeros_like(q2))], axis=0)

    def step(kj, diagonal):
        start = pl.multiple_of(kj * tile, tile)
        kt = k_ref[0, pl.ds(start, tile), :]
        vt = v_ref[0, pl.ds(start, tile), :]
        z = lax.dot_general(qs, kt, (((1,), (1,)), ((), ())), preferred_element_type=F32)
        sp = _softplus2(z)
        if diagonal:
            sp = jnp.where(causal, sp, 0.0)
        r = _suffix_sums(sp, later_keys)
        logit = z - sp - r[:, :tile]
        total = r[:, tile:]
        if diagonal:
            a = jnp.where(causal, jnp.exp2(logit), 0.0)
        else:
            later = later_ref[...]
            a = jnp.exp2(logit - jnp.concatenate([later] * (tile // LANES), axis=1))
        pv = jnp.dot(a.astype(vt.dtype), vt, preferred_element_type=F32)
        if diagonal:
            acc_ref[...] = pv
            later_ref[...] = total
        else:
            acc_ref[...] += pv
            later_ref[...] = later + total

    step(qi, True)

    def more(state):
        j, live = state
        return (j < qi) & (live > 0)

    def body(state):
        j, _ = state
        step(qi - 1 - j, False)
        reach = jnp.min(later_ref[...])
        return j + 1, (reach < ATTN_EXIT).astype(jnp.int32)

    lax.while_loop(more, body, (jnp.int32(0), jnp.int32(1)))
    o_ref[0, rows, :] = jnp.where(lane < SB_HEAD_DIM, acc_ref[:tile], acc_ref[tile:]).astype(o_ref.dtype)


def _attn_call(q, k, v):
    b, lp, d = q.shape
    tile = ATTN_TILE
    j = jnp.arange(tile)
    strict = (j[:, None] > j[None, :]).astype(F32)
    later_keys = jnp.concatenate([strict, jnp.ones((tile, LANES), F32)], axis=1).astype(MXU_DTYPE)
    seq = pl.BlockSpec((1, lp, LANES), lambda bi, hp: (bi, 0, hp))
    return pl.pallas_call(
        functools.partial(_attn_kernel, tile=tile),
        grid=(b, d // LANES),
        in_specs=[seq, seq, seq, pl.BlockSpec((tile, tile + LANES), lambda bi, hp: (0, 0))],
        out_specs=seq,
        out_shape=jax.ShapeDtypeStruct((b, lp, d), MXU_DTYPE),
        scratch_shapes=[pltpu.VMEM((2 * tile, LANES), F32),
                        pltpu.VMEM((2 * tile, LANES), F32)],
        compiler_params=_params("arbitrary", "arbitrary"),
        name="sb_attention",
    )(q, k, v, later_keys)


def _proj_kernel(h_ref, o_ref, w_ref, out_ref):
    out_ref[...] = h_ref[...] + jnp.dot(o_ref[...], w_ref[...], preferred_element_type=F32)


def _proj_call(h2d, o2d, w_o):
    t, d = h2d.shape
    row = pl.BlockSpec((ROW_TILE, d), lambda i: (i, 0))
    return pl.pallas_call(
        _proj_kernel,
        grid=(t // ROW_TILE,),
        in_specs=[row, row, pl.BlockSpec((d, d), lambda i: (0, 0))],
        out_specs=row,
        out_shape=jax.ShapeDtypeStruct((t, d), F32),
        compiler_params=_params("arbitrary"),
        name="attn_out_proj",
    )(h2d, o2d, w_o.astype(MXU_DTYPE))


def _pool_kernel(h_ref, halo_ref, g_ref, w_ref, s_ref, out_ref, ext_ref, *, ts):
    i = pl.program_id(1)
    x = h_ref[0]
    gain = g_ref[...]
    hn = _rms(x, gain)
    halo = jnp.where(i > 0, _rms(halo_ref[0], gain), 0.0)
    ext_ref[0:POOL_HALO, :] = halo
    ext_ref[POOL_HALO:, :] = hn
    pos = i * ts + lax.broadcasted_iota(jnp.int32, (ts, 1), 0)
    for g, w in enumerate(POOL_WINDOWS):
        sl = slice(g * POOL_GROUP, (g + 1) * POOL_GROUP)
        acc = ext_ref[POOL_HALO:, sl]
        for kk in range(1, w):
            acc = acc + ext_ref[POOL_HALO - kk:POOL_HALO - kk + ts, sl]
        cnt = jnp.minimum(pos + 1, w).astype(F32)
        pooled = acc / cnt - ext_ref[POOL_HALO:, sl]
        y = jnp.dot(pooled.astype(MXU_DTYPE), w_ref[g], preferred_element_type=F32)
        out_ref[0, :, sl] = x[:, sl] + y * s_ref[:, sl]


def _pool_call(h, gain, pool_w, pool_scale):
    b, lp, d = h.shape
    ts = next(m * ATTN_TILE for m in (4, 3, 2, 1) if lp % (m * ATTN_TILE) == 0)
    per = ts // POOL_HALO
    ng = len(POOL_WINDOWS)
    return pl.pallas_call(
        functools.partial(_pool_kernel, ts=ts),
        grid=(b, lp // ts),
        in_specs=[
            pl.BlockSpec((1, ts, d), lambda bi, i: (bi, i, 0)),
            pl.BlockSpec((1, POOL_HALO, d), lambda bi, i: (bi, jnp.maximum(i * per - 1, 0), 0)),
            pl.BlockSpec((1, d), lambda bi, i: (0, 0)),
            pl.BlockSpec((ng, POOL_GROUP, POOL_GROUP), lambda bi, i: (0, 0, 0)),
            pl.BlockSpec((1, d), lambda bi, i: (0, 0)),
        ],
        out_specs=pl.BlockSpec((1, ts, d), lambda bi, i: (bi, i, 0)),
        out_shape=jax.ShapeDtypeStruct((b, lp, d), F32),
        scratch_shapes=[pltpu.VMEM((POOL_HALO + ts, d), F32)],
        compiler_params=_params("arbitrary", "arbitrary"),
        name="pool_mixer",
    )(h, h, gain[None, :].astype(F32), pool_w.astype(MXU_DTYPE), pool_scale[None, :].astype(F32))


_PEER_RANKS = PEER_TOPK + 1
_PEER_PAIRS = tuple((a, b) for a in range(_PEER_RANKS) for b in range(_PEER_RANKS)
                    if (a + 1) * (b + 1) <= _PEER_RANKS)
_PEER_CAND_ROWS = -(-len(_PEER_PAIRS) // 8) * 8
_PEER_RANK_ROWS = -(-_PEER_RANKS // 8) * 8


def _sorting_network(n):
    def merge(lo, hi, r):
        step = r * 2
        if step < hi - lo:
            yield from merge(lo, hi, step)
            yield from merge(lo + r, hi, step)
            yield from ((i, i + r) for i in range(lo + r, hi - r, step))
        else:
            yield (lo, lo + r)

    def sort(lo, hi):
        if hi > lo:
            mid = lo + (hi - lo) // 2
            yield from sort(lo, mid)
            yield from sort(mid + 1, hi)
            yield from merge(lo, hi, 1)

    return tuple(sort(0, n - 1))


def _largest_per_column(src_ref, lanes, dst_ref, n):
    groups = src_ref.shape[0] // SUBLANES
    size = 1 << (groups - 1).bit_length()
    lowest = jnp.full((SUBLANES, LANES), -jnp.inf, F32)
    s = [src_ref[g * SUBLANES:(g + 1) * SUBLANES, lanes] for g in range(groups)]
    s += [lowest] * (size - groups)
    for i, j in _sorting_network(size):
        s[i], s[j] = jnp.maximum(s[i], s[j]), jnp.minimum(s[i], s[j])
    s.append(lowest)
    for r in range(n):
        m = jnp.max(s[0], axis=0, keepdims=True)
        dst_ref[r:r + 1, lanes] = m
        if r + 1 < n:
            hit = s[0] == m
            for k in range(min(n - 1 - r, size)):
                s[k] = jnp.where(hit, s[k + 1], s[k])


def _peer_kernel(h_ref, g_ref, wq_ref, sk_ref, u_ref, vt_ref, out_ref,
                 hn_ref, e2_ref, e1_ref, phi_ref, acc_ref, p_ref, pre_ref,
                 s1_ref, s2_ref, t1_ref, t2_ref, cand_ref, top_ref):
    e = pl.program_id(1)
    nk = PEER_N_KEYS

    @pl.when(e == 0)
    def _prepare():
        hn_ref[...] = _rms(h_ref[...], g_ref[...]).astype(MXU_DTYPE).T
        acc_ref[...] = jnp.zeros_like(acc_ref)

        def head(hd, q0):
            q1 = pre_ref[pl.ds(pl.multiple_of(q0, nk), nk), :].astype(MXU_DTYPE)
            q2 = pre_ref[pl.ds(pl.multiple_of(q0 + nk, nk), nk), :].astype(MXU_DTYPE)
            s1_ref[...] = jnp.dot(sk_ref[0], q1, preferred_element_type=F32)
            s2_ref[...] = jnp.dot(sk_ref[1], q2, preferred_element_type=F32)
            strips = [slice(cs * LANES, (cs + 1) * LANES) for cs in range(s1_ref.shape[1] // LANES)]
            for lanes in strips:
                _largest_per_column(s1_ref, lanes, t1_ref, _PEER_RANKS)
                _largest_per_column(s2_ref, lanes, t2_ref, _PEER_RANKS)
            cand_ref[...] = jnp.full(cand_ref.shape, -jnp.inf, F32)
            for idx, (a, b) in enumerate(_PEER_PAIRS):
                cand_ref[idx:idx + 1, :] = t1_ref[a:a + 1, :] + t2_ref[b:b + 1, :]
            for lanes in strips:
                _largest_per_column(cand_ref, lanes, top_ref, _PEER_RANKS)
            s1 = s1_ref[...]
            s2 = s2_ref[...]
            top = top_ref[...]
            best = top[0:1, :]
            z = jnp.sum(jnp.exp(top[0:PEER_TOPK, :] - best), axis=0, keepdims=True)
            tau = 0.5 * (top[PEER_TOPK - 1:PEER_TOPK, :] + top[PEER_TOPK:PEER_TOPK + 1, :])
            phi_ref[pl.ds(hd, 1), :] = jnp.exp(tau - best) / z
            e1 = jnp.exp(s1 - t1_ref[0:1, :])
            for blk in range(nk // SUBLANES):
                e1_ref[hd, blk] = e1[blk * SUBLANES:(blk + 1) * SUBLANES, :]
            e2_ref[hd] = (jnp.exp(s2 - t2_ref[0:1, :]) / z).astype(e2_ref.dtype)

        rows = pre_ref.shape[0]
        per_group = rows // (2 * nk)
        for grp in range(PEER_HEADS // per_group):
            pre_ref[...] = jnp.dot(wq_ref[grp * rows:(grp + 1) * rows, :], hn_ref[...],
                                   preferred_element_type=F32)

            def body(k, carry, grp=grp):
                head(grp * per_group + k, k * 2 * nk)
                return carry

            lax.fori_loop(0, per_group, body, 0)

    rows_per = 2 * SUBLANES

    pre_ref[...] = jnp.dot(u_ref[...], hn_ref[...], preferred_element_type=F32)
    for cs in range(hn_ref.shape[1] // LANES):
        lanes = slice(cs * LANES, (cs + 1) * LANES)
        phi_rows = [jnp.broadcast_to(phi_ref[hd:hd + 1, lanes], (rows_per, LANES))
                    .astype(e2_ref.dtype) for hd in range(PEER_HEADS)]
        for ii in range(SUBLANES):
            e1_rows = [jnp.broadcast_to(e1_ref[hd, e, ii:ii + 1, lanes], (rows_per, LANES))
                       .astype(e2_ref.dtype) for hd in range(PEER_HEADS)]
            for jp in range(nk // rows_per):
                w = None
                for hd in range(PEER_HEADS):
                    g = e2_ref[hd, jp * rows_per:(jp + 1) * rows_per, lanes] * e1_rows[hd]
                    term = jnp.where(g >= phi_rows[hd], g, jnp.zeros_like(g))
                    w = term if w is None else w + term
                rows = slice(ii * nk + jp * rows_per, ii * nk + (jp + 1) * rows_per)
                pre = pre_ref[rows, lanes]
                act = 0.5 * pre * (1.0 + lax.erf(pre * math.sqrt(0.5)))
                p_ref[rows, lanes] = w * act.astype(w.dtype)
    acc_ref[...] += jnp.dot(vt_ref[0], p_ref[...], preferred_element_type=F32)

    @pl.when(e == pl.num_programs(1) - 1)
    def _finish():
        out_ref[...] = h_ref[...] + acc_ref[...].T


def _peer_call(h2d, gain, w_q, subkeys, u_tab, v_tab):
    t, d = h2d.shape
    n_exp = u_tab.shape[0]
    c, te = PEER_TOKENS, PEER_EXPERTS
    nk = PEER_N_KEYS
    n_blocks = n_exp // te
    once = pl.Buffered(1)
    vt_blocks = v_tab.astype(MXU_DTYPE).reshape(n_blocks, te, d).transpose(0, 2, 1)
    return pl.pallas_call(
        _peer_kernel,
        grid=(t // c, n_blocks),
        in_specs=[
            pl.BlockSpec((c, d), lambda ti, e: (ti, 0), pipeline_mode=once),
            pl.BlockSpec((1, d), lambda ti, e: (0, 0)),
            pl.BlockSpec((PEER_HEADS * 2 * nk, d), lambda ti, e: (0, 0), pipeline_mode=once),
            pl.BlockSpec((2, nk, nk), lambda ti, e: (0, 0, 0)),
            pl.BlockSpec((te, d), lambda ti, e: (e, 0)),
            pl.BlockSpec((1, d, te), lambda ti, e: (e, 0, 0)),
        ],
        out_specs=pl.BlockSpec((c, d), lambda ti, e: (ti, 0)),
        out_shape=jax.ShapeDtypeStruct((t, d), F32),
        scratch_shapes=[
            pltpu.VMEM((d, c), MXU_DTYPE),
            pltpu.VMEM((PEER_HEADS, nk, c), MXU_DTYPE),
            pltpu.VMEM((PEER_HEADS, nk // SUBLANES, SUBLANES, c), F32),
            pltpu.VMEM((PEER_HEADS, c), F32),
            pltpu.VMEM((d, c), F32),
            pltpu.VMEM((te, c), MXU_DTYPE),
            pltpu.VMEM((te, c), F32),
            pltpu.VMEM((nk, c), F32),
            pltpu.VMEM((nk, c), F32),
            pltpu.VMEM((_PEER_RANK_ROWS, c), F32),
            pltpu.VMEM((_PEER_RANK_ROWS, c), F32),
            pltpu.VMEM((_PEER_CAND_ROWS, c), F32),
            pltpu.VMEM((_PEER_RANK_ROWS, c), F32),
        ],
        compiler_params=_params("arbitrary", "arbitrary"),
        name="peer_ffn",
    )(h2d, gain[None, :].astype(F32), w_q.T.astype(MXU_DTYPE), subkeys.astype(MXU_DTYPE),
      u_tab.astype(MXU_DTYPE), vt_blocks)


def kernel(x, meta, norm_mix, norm_ffn, sb_w_qkv, sb_q_gain, sb_k_gain, sb_w_o, pool_w, pool_scale,
           peer_w_q, peer_subkeys, peer_u, peer_v):
    b, seq, d = x.shape
    depth = norm_mix.shape[0]
    l = N_META + seq
    lp = -(-l // ATTN_TILE) * ATTN_TILE
    while (b * lp) % math.lcm(ROW_TILE, PEER_TOKENS):
        lp += ATTN_TILE
    h = jnp.concatenate([jnp.broadcast_to(meta[None].astype(x.dtype), (b, N_META, d)), x,
                         jnp.zeros((b, lp - l, d), x.dtype)], axis=1)
    for i in range(depth):
        j = i // 2
        if i % 2 == 0:
            h2d = h.reshape(b * lp, d)
            q, k, v = _qkv_call(h2d, norm_mix[i], sb_w_qkv[j], sb_q_gain[j], sb_k_gain[j])
            o = _attn_call(q.reshape(b, lp, d), k.reshape(b, lp, d), v.reshape(b, lp, d))
            h2d = _proj_call(h2d, o.reshape(b * lp, d), sb_w_o[j])
        else:
            h2d = _pool_call(h, norm_mix[i], pool_w[j], pool_scale[j]).reshape(b * lp, d)
        h2d = _peer_call(h2d, norm_ffn[i], peer_w_q[i], peer_subkeys[i], peer_u[i], peer_v[i])
        h = h2d.reshape(b, lp, d)
    return h[:, N_META:l]
```

```python
import functools
import math

import jax
import jax.numpy as jnp
from jax import lax
from jax.experimental import pallas as pl
from jax.experimental.pallas import tpu as pltpu

F32 = jnp.float32
MXU_DTYPE = jnp.bfloat16

RMS_EPS = 1e-6
N_META = 16
SB_HEAD_DIM = 64
POOL_WINDOWS = (2, 4, 8, 16)
POOL_GROUP = 256
PEER_HEADS = 8
PEER_N_KEYS = 128
PEER_TOPK = 16

LANES = 128
SUBLANES = 8
ATTN_TILE = 256
ROW_TILE = 1024
PEER_TOKENS = 1024
PEER_EXPERTS = SUBLANES * PEER_N_KEYS
POOL_HALO = 16
VMEM_LIMIT = 56 * 1024 * 1024


def _rms(x, gain):
    ms = jnp.mean(x * x, axis=-1, keepdims=True)
    return x * lax.rsqrt(ms + RMS_EPS) * gain


def _params(*sem):
    return pltpu.CompilerParams(dimension_semantics=sem, vmem_limit_bytes=VMEM_LIMIT)


def _qkv_kernel(h_ref, g_ref, w_ref, bd_ref, qg_ref, kg_ref, q_ref, k_ref, v_ref):
    d = h_ref.shape[1]
    hn = _rms(h_ref[...], g_ref[...]).astype(MXU_DTYPE)
    bd = bd_ref[...]
    for src, gref, dst in ((0, qg_ref, q_ref), (1, kg_ref, k_ref)):
        t = jnp.dot(hn, w_ref[:, src * d:(src + 1) * d], preferred_element_type=F32)
        for c in range(d // LANES):
            sl = slice(c * LANES, (c + 1) * LANES)
            tc = t[:, sl]
            ms = jnp.dot((tc * tc).astype(MXU_DTYPE), bd, preferred_element_type=F32)
            dst[:, sl] = (tc * lax.rsqrt(ms + RMS_EPS) * gref[:, sl]).astype(dst.dtype)
    v_ref[...] = jnp.dot(hn, w_ref[:, 2 * d:], preferred_element_type=F32).astype(v_ref.dtype)


def _qkv_call(h2d, gain, w_qkv, q_gain, k_gain):
    t, d = h2d.shape
    scale = LOG2E / math.sqrt(SB_HEAD_DIM)
    reps = d // SB_HEAD_DIM
    qg = jnp.tile(q_gain.astype(F32) * scale, reps)[None, :]
    kg = jnp.tile(k_gain.astype(F32), reps)[None, :]
    idx = jnp.arange(LANES) // SB_HEAD_DIM
    bd = (idx[:, None] == idx[None, :]).astype(F32) / SB_HEAD_DIM
    row = pl.BlockSpec((ROW_TILE, d), lambda i: (i, 0))
    full = lambda shape: pl.BlockSpec(shape, lambda i: (0,) * len(shape))
    out = jax.ShapeDtypeStruct((t, d), MXU_DTYPE)
    return pl.pallas_call(
        _qkv_kernel,
        grid=(t // ROW_TILE,),
        in_specs=[row, full((1, d)), full((d, 3 * d)), full((LANES, LANES)), full((1, d)), full((1, d))],
        out_specs=[row, row, row],
        out_shape=[out, out, out],
        compiler_params=_params("arbitrary"),
        name="qkv_proj",
    )(h2d, gain[None, :].astype(F32), w_qkv.astype(MXU_DTYPE), bd.astype(MXU_DTYPE), qg, kg)


LOG2E = math.log2(math.e)
ATTN_EXIT = 160.0


def _softplus2(z):
    return jnp.maximum(z, 0.0) + jnp.log(1.0 + jnp.exp2(-jnp.abs(z))) * LOG2E


def _suffix_sums(sp, later_keys):
    rows = sp.shape[0]
    hi = sp.astype(MXU_DTYPE)
    lo = (sp - hi.astype(F32)).astype(MXU_DTYPE)
    r = jnp.dot(jnp.concatenate([hi, lo], axis=0), later_keys, preferred_element_type=F32)
    return r[:rows] + r[rows:]


def _attn_kernel(q_ref, k_ref, v_ref, lk_ref, o_ref, acc_ref, later_ref, *, tile):
    lane = lax.broadcasted_iota(jnp.int32, (1, LANES), 1)
    row = lax.broadcasted_iota(jnp.int32, (2 * tile, tile), 0)
    col = lax.broadcasted_iota(jnp.int32, (2 * tile, tile), 1)
    causal = col < (row & (tile - 1))
    later_keys = lk_ref[...]

    def query_tile(qi, carry):
        _attn_query_tile(qi, q_ref, k_ref, v_ref, o_ref, acc_ref, later_ref,
                         lane, causal, later_keys, tile)
        return carry

    lax.fori_loop(0, q_ref.shape[1] // tile, query_tile, 0)


def _attn_query_tile(qi, q_ref, k_ref, v_ref, o_ref, acc_ref, later_ref, lane, causal, later_keys, tile):
    rows = pl.ds(pl.multiple_of(qi * tile, tile), tile)
    q2 = q_ref[0, rows, :]
    qs = jnp.concatenate([jnp.where(lane < SB_HEAD_DIM, q2, jnp.zeros_like(q2)),
                          jnp.where(lane >= SB_HEAD_DIM, q2, jnp.zeros_like(q2))], axis=0)

    def step(kj, diagonal):
        start = pl.multiple_of(kj * tile, tile)
        kt = k_ref[0, pl.ds(start, tile), :]
        vt = v_ref[0, pl.ds(start, tile), :]
        z = lax.dot_general(qs, kt, (((1,), (1,)), ((), ())), preferred_element_type=F32)
        sp = _softplus2(z)
        if diagonal:
            sp = jnp.where(causal, sp, 0.0)
        r = _suffix_sums(sp, later_keys)
        logit = z - sp - r[:, :tile]
        total = r[:, tile:]
        if diagonal:
            a = jnp.where(causal, jnp.exp2(logit), 0.0)
        else:
            later = later_ref[...]
            a = jnp.exp2(logit - jnp.concatenate([later] * (tile // LANES), axis=1))
        pv = jnp.dot(a.astype(vt.dtype), vt, preferred_element_type=F32)
        if diagonal:
            acc_ref[...] = pv
            later_ref[...] = total
        else:
            acc_ref[...] += pv
            later_ref[...] = later + total

    step(qi, True)

    def more(state):
        j, live = state
        return (j < qi) & (live > 0)

    def body(state):
        j, _ = state
        step(qi - 1 - j, False)
        reach = jnp.min(later_ref[...])
        return j + 1, (reach < ATTN_EXIT).astype(jnp.int32)

    lax.while_loop(more, body, (jnp.int32(0), jnp.int32(1)))
    o_ref[0, rows, :] = jnp.where(lane < SB_HEAD_DIM, acc_ref[:tile], acc_ref[tile:]).astype(o_ref.dtype)


def _attn_call(q, k, v):
    b, lp, d = q.shape
    tile = ATTN_TILE
    j = jnp.arange(tile)
    strict = (j[:, None] > j[None, :]).astype(F32)
    later_keys = jnp.concatenate([strict, jnp.ones((tile, LANES), F32)], axis=1).astype(MXU_DTYPE)
    seq = pl.BlockSpec((1, lp, LANES), lambda bi, hp: (bi, 0, hp))
    return pl.pallas_call(
        functools.partial(_attn_kernel, tile=tile),
        grid=(b, d // LANES),
        in_specs=[seq, seq, seq, pl.BlockSpec((tile, tile + LANES), lambda bi, hp: (0, 0))],
        out_specs=seq,
        out_shape=jax.ShapeDtypeStruct((b, lp, d), MXU_DTYPE),
        scratch_shapes=[pltpu.VMEM((2 * tile, LANES), F32),
                        pltpu.VMEM((2 * tile, LANES), F32)],
        compiler_params=_params("arbitrary", "arbitrary"),
        name="sb_attention",
    )(q, k, v, later_keys)


def _proj_kernel(h_ref, o_ref, w_ref, out_ref):
    out_ref[...] = h_ref[...] + jnp.dot(o_ref[...], w_ref[...], preferred_element_type=F32)


def _proj_call(h2d, o2d, w_o):
    t, d = h2d.shape
    row = pl.BlockSpec((ROW_TILE, d), lambda i: (i, 0))
    return pl.pallas_call(
        _proj_kernel,
        grid=(t // ROW_TILE,),
        in_specs=[row, row, pl.BlockSpec((d, d), lambda i: (0, 0))],
        out_specs=row,
        out_shape=jax.ShapeDtypeStruct((t, d), F32),
        compiler_params=_params("arbitrary"),
        name="attn_out_proj",
    )(h2d, o2d, w_o.astype(MXU_DTYPE))


def _pool_kernel(h_ref, halo_ref, g_ref, w_ref, s_ref, out_ref, ext_ref, *, ts):
    i = pl.program_id(1)
    x = h_ref[0]
    gain = g_ref[...]
    hn = _rms(x, gain)
    halo = jnp.where(i > 0, _rms(halo_ref[0], gain), 0.0)
    ext_ref[0:POOL_HALO, :] = halo
    ext_ref[POOL_HALO:, :] = hn
    pos = i * ts + lax.broadcasted_iota(jnp.int32, (ts, 1), 0)
    for g, w in enumerate(POOL_WINDOWS):
        sl = slice(g * POOL_GROUP, (g + 1) * POOL_GROUP)
        acc = ext_ref[POOL_HALO:, sl]
        for kk in range(1, w):
            acc = acc + ext_ref[POOL_HALO - kk:POOL_HALO - kk + ts, sl]
        cnt = jnp.minimum(pos + 1, w).astype(F32)
        pooled = acc / cnt - ext_ref[POOL_HALO:, sl]
        y = jnp.dot(pooled.astype(MXU_DTYPE), w_ref[g], preferred_element_type=F32)
        out_ref[0, :, sl] = x[:, sl] + y * s_ref[:, sl]


def _pool_call(h, gain, pool_w, pool_scale):
    b, lp, d = h.shape
    ts = next(m * ATTN_TILE for m in (4, 3, 2, 1) if lp % (m * ATTN_TILE) == 0)
    per = ts // POOL_HALO
    ng = len(POOL_WINDOWS)
    return pl.pallas_call(
        functools.partial(_pool_kernel, ts=ts),
        grid=(b, lp // ts),
        in_specs=[
            pl.BlockSpec((1, ts, d), lambda bi, i: (bi, i, 0)),
            pl.BlockSpec((1, POOL_HALO, d), lambda bi, i: (bi, jnp.maximum(i * per - 1, 0), 0)),
            pl.BlockSpec((1, d), lambda bi, i: (0, 0)),
            pl.BlockSpec((ng, POOL_GROUP, POOL_GROUP), lambda bi, i: (0, 0, 0)),
            pl.BlockSpec((1, d), lambda bi, i: (0, 0)),
        ],
        out_specs=pl.BlockSpec((1, ts, d), lambda bi, i: (bi, i, 0)),
        out_shape=jax.ShapeDtypeStruct((b, lp, d), F32),
        scratch_shapes=[pltpu.VMEM((POOL_HALO + ts, d), F32)],
        compiler_params=_params("arbitrary", "arbitrary"),
        name="pool_mixer",
    )(h, h, gain[None, :].astype(F32), pool_w.astype(MXU_DTYPE), pool_scale[None, :].astype(F32))


_PEER_RANKS = PEER_TOPK + 1
_PEER_PAIRS = tuple((a, b) for a in range(_PEER_RANKS) for b in range(_PEER_RANKS)
                    if (a + 1) * (b + 1) <= _PEER_RANKS)
_PEER_CAND_ROWS = -(-len(_PEER_PAIRS) // 8) * 8
_PEER_RANK_ROWS = -(-_PEER_RANKS // 8) * 8


def _sorting_network(n):
    def merge(lo, hi, r):
        step = r * 2
        if step < hi - lo:
            yield from merge(lo, hi, step)
            yield from merge(lo + r, hi, step)
            yield from ((i, i + r) for i in range(lo + r, hi - r, step))
        else:
            yield (lo, lo + r)

    def sort(lo, hi):
        if hi > lo:
            mid = lo + (hi - lo) // 2
            yield from sort(lo, mid)
            yield from sort(mid + 1, hi)
            yield from merge(lo, hi, 1)

    return tuple(sort(0, n - 1))


def _largest_per_column(src_ref, lanes, dst_ref, n):
    groups = src_ref.shape[0] // SUBLANES
    size = 1 << (groups - 1).bit_length()
    lowest = jnp.full((SUBLANES, LANES), -jnp.inf, F32)
    s = [src_ref[g * SUBLANES:(g + 1) * SUBLANES, lanes] for g in range(groups)]
    s += [lowest] * (size - groups)
    for i, j in _sorting_network(size):
        s[i], s[j] = jnp.maximum(s[i], s[j]), jnp.minimum(s[i], s[j])
    s.append(lowest)
    for r in range(n):
        m = jnp.max(s[0], axis=0, keepdims=True)
        dst_ref[r:r + 1, lanes] = m
        if r + 1 < n:
            hit = s[0] == m
            for k in range(min(n - 1 - r, size)):
                s[k] = jnp.where(hit, s[k + 1], s[k])


def _peer_kernel(h_ref, g_ref, wq_ref, sk_ref, u_ref, vt_ref, out_ref,
                 hn_ref, e2_ref, e1_ref, phi_ref, acc_ref, p_ref, pre_ref,
                 s1_ref, s2_ref, t1_ref, t2_ref, cand_ref, top_ref):
    e = pl.program_id(1)
    nk = PEER_N_KEYS

    @pl.when(e == 0)
    def _prepare():
        hn_ref[...] = _rms(h_ref[...], g_ref[...]).astype(MXU_DTYPE).T
        acc_ref[...] = jnp.zeros_like(acc_ref)

        def head(hd, q0):
            q1 = pre_ref[pl.ds(pl.multiple_of(q0, nk), nk), :].astype(MXU_DTYPE)
            q2 = pre_ref[pl.ds(pl.multiple_of(q0 + nk, nk), nk), :].astype(MXU_DTYPE)
            s1_ref[...] = jnp.dot(sk_ref[0], q1, preferred_element_type=F32)
            s2_ref[...] = jnp.dot(sk_ref[1], q2, preferred_element_type=F32)
            strips = [slice(cs * LANES, (cs + 1) * LANES) for cs in range(s1_ref.shape[1] // LANES)]
            for lanes in strips:
                _largest_per_column(s1_ref, lanes, t1_ref, _PEER_RANKS)
                _largest_per_column(s2_ref, lanes, t2_ref, _PEER_RANKS)
            cand_ref[...] = jnp.full(cand_ref.shape, -jnp.inf, F32)
            for idx, (a, b) in enumerate(_PEER_PAIRS):
                cand_ref[idx:idx + 1, :] = t1_ref[a:a + 1, :] + t2_ref[b:b + 1, :]
            for lanes in strips:
                _largest_per_column(cand_ref, lanes, top_ref, _PEER_RANKS)
            s1 = s1_ref[...]
            s2 = s2_ref[...]
            top = top_ref[...]
            best = top[0:1, :]
            z = jnp.sum(jnp.exp(top[0:PEER_TOPK, :] - best), axis=0, keepdims=True)
            tau = 0.5 * (top[PEER_TOPK - 1:PEER_TOPK, :] + top[PEER_TOPK:PEER_TOPK + 1, :])
            half_z = 0.5 / z
            phi_ref[pl.ds(hd, 1), :] = jnp.exp(tau - best) * half_z
            e1 = jnp.exp(s1 - t1_ref[0:1, :])
            for blk in range(nk // SUBLANES):
                e1_ref[hd, blk] = e1[blk * SUBLANES:(blk + 1) * SUBLANES, :]
            e2_ref[hd] = (jnp.exp(s2 - t2_ref[0:1, :]) * half_z).astype(e2_ref.dtype)

        rows = pre_ref.shape[0]
        per_group = rows // (2 * nk)
        for grp in range(PEER_HEADS // per_group):
            pre_ref[...] = jnp.dot(wq_ref[grp * rows:(grp + 1) * rows, :], hn_ref[...],
                                   preferred_element_type=F32)

            def body(k, carry, grp=grp):
                head(grp * per_group + k, k * 2 * nk)
                return carry

            lax.fori_loop(0, per_group, body, 0)

    rows_per = 2 * SUBLANES

    pre_ref[...] = jnp.dot(u_ref[...], hn_ref[...], preferred_element_type=F32)
    for cs in range(hn_ref.shape[1] // LANES):
        lanes = slice(cs * LANES, (cs + 1) * LANES)
        phi_rows = [jnp.broadcast_to(phi_ref[hd:hd + 1, lanes], (rows_per, LANES))
                    .astype(e2_ref.dtype) for hd in range(PEER_HEADS)]
        for ii in range(SUBLANES):
            e1_rows = [jnp.broadcast_to(e1_ref[hd, e, ii:ii + 1, lanes], (rows_per, LANES))
                       .astype(e2_ref.dtype) for hd in range(PEER_HEADS)]
            for jp in range(nk // rows_per):
                w = None
                for hd in range(PEER_HEADS):
                    g = e2_ref[hd, jp * rows_per:(jp + 1) * rows_per, lanes] * e1_rows[hd]
                    term = jnp.where(g >= phi_rows[hd], g, jnp.zeros_like(g))
                    w = term if w is None else w + term
                rows = slice(ii * nk + jp * rows_per, ii * nk + (jp + 1) * rows_per)
                pre = pre_ref[rows, lanes]
                act = pre * (1.0 + lax.erf(pre * math.sqrt(0.5)))
                p_ref[rows, lanes] = w * act.astype(w.dtype)
    acc_ref[...] += jnp.dot(vt_ref[0], p_ref[...], preferred_element_type=F32)

    @pl.when(e == pl.num_programs(1) - 1)
    def _finish():
        out_ref[...] = h_ref[...] + acc_ref[...].T


def _peer_call(h2d, gain, w_q, subkeys, u_tab, v_tab):
    t, d = h2d.shape
    n_exp = u_tab.shape[0]
    c, te = PEER_TOKENS, PEER_EXPERTS
    nk = PEER_N_KEYS
    n_blocks = n_exp // te
    once = pl.Buffered(1)
    vt_blocks = v_tab.astype(MXU_DTYPE).reshape(n_blocks, te, d).transpose(0, 2, 1)
    return pl.pallas_call(
        _peer_kernel,
        grid=(t // c, n_blocks),
        in_specs=[
            pl.BlockSpec((c, d), lambda ti, e: (ti, 0), pipeline_mode=once),
            pl.BlockSpec((1, d), lambda ti, e: (0, 0)),
            pl.BlockSpec((PEER_HEADS * 2 * nk, d), lambda ti, e: (0, 0), pipeline_mode=once),
            pl.BlockSpec((2, nk, nk), lambda ti, e: (0, 0, 0)),
            pl.BlockSpec((te, d), lambda ti, e: (e, 0)),
            pl.BlockSpec((1, d, te), lambda ti, e: (e, 0, 0)),
        ],
        out_specs=pl.BlockSpec((c, d), lambda ti, e: (ti, 0)),
        out_shape=jax.ShapeDtypeStruct((t, d), F32),
        scratch_shapes=[
            pltpu.VMEM((d, c), MXU_DTYPE),
            pltpu.VMEM((PEER_HEADS, nk, c), MXU_DTYPE),
            pltpu.VMEM((PEER_HEADS, nk // SUBLANES, SUBLANES, c), F32),
            pltpu.VMEM((PEER_HEADS, c), F32),
            pltpu.VMEM((d, c), F32),
            pltpu.VMEM((te, c), MXU_DTYPE),
            pltpu.VMEM((te, c), F32),
            pltpu.VMEM((nk, c), F32),
            pltpu.VMEM((nk, c), F32),
            pltpu.VMEM((_PEER_RANK_ROWS, c), F32),
            pltpu.VMEM((_PEER_RANK_ROWS, c), F32),
            pltpu.VMEM((_PEER_CAND_ROWS, c), F32),
            pltpu.VMEM((_PEER_RANK_ROWS, c), F32),
        ],
        compiler_params=_params("arbitrary", "arbitrary"),
        name="peer_ffn",
    )(h2d, gain[None, :].astype(F32), w_q.T.astype(MXU_DTYPE), subkeys.astype(MXU_DTYPE),
      u_tab.astype(MXU_DTYPE), vt_blocks)


def kernel(x, meta, norm_mix, norm_ffn, sb_w_qkv, sb_q_gain, sb_k_gain, sb_w_o, pool_w, pool_scale,
           peer_w_q, peer_subkeys, peer_u, peer_v):
    b, seq, d = x.shape
    depth = norm_mix.shape[0]
    l = N_META + seq
    lp = -(-l // ATTN_TILE) * ATTN_TILE
    while (b * lp) % math.lcm(ROW_TILE, PEER_TOKENS):
        lp += ATTN_TILE
    h = jnp.concatenate([jnp.broadcast_to(meta[None].astype(x.dtype), (b, N_META, d)), x,
                         jnp.zeros((b, lp - l, d), x.dtype)], axis=1)
    for i in range(depth):
        j = i // 2
        if i % 2 == 0:
            h2d = h.reshape(b * lp, d)
            q, k, v = _qkv_call(h2d, norm_mix[i], sb_w_qkv[j], sb_q_gain[j], sb_k_gain[j])
            o = _attn_call(q.reshape(b, lp, d), k.reshape(b, lp, d), v.reshape(b, lp, d))
            h2d = _proj_call(h2d, o.reshape(b * lp, d), sb_w_o[j])
        else:
            h2d = _pool_call(h, norm_mix[i], pool_w[j], pool_scale[j]).reshape(b * lp, d)
        h2d = _peer_call(h2d, norm_ffn[i], peer_w_q[i], peer_subkeys[i], peer_u[i], peer_v[i])
        h = h2d.reshape(b, lp, d)
    return h[:, N_META:l]
```
